```python
import math
import jax, jax.numpy as jnp
from jax import lax
import numpy as np

D_MODEL = 1024
BATCH = 8
SEQ = 2048
DEPTH = 2
DEC_BATCH = 32
DEC_SEQ = 4
PAST_LEN = 16384
PAGE_SIZE = 128

N_HEADS_DIFF = 8
DIFF_HEAD_DIM = D_MODEL // N_HEADS_DIFF // 2
DIFF_V_DIM = 2 * DIFF_HEAD_DIM
DIFF_Q_DIM = N_HEADS_DIFF * 2 * DIFF_HEAD_DIM
DIFF_VALS_DIM = N_HEADS_DIFF * DIFF_V_DIM
N_HEADS_RET = 4
RET_QK_DIM = D_MODEL // N_HEADS_RET
RET_V_DIM = 2 * RET_QK_DIM
RET_QK_TOTAL = N_HEADS_RET * RET_QK_DIM
RET_V_TOTAL = N_HEADS_RET * RET_V_DIM
D_FF = 4 * D_MODEL
Q_BLOCK = 128
RET_CHUNK = 128
N_ATTN_LAYERS = (DEPTH + 1) // 2
N_RET_LAYERS = DEPTH // 2
EPS = 1e-6
LAMBDA_INIT_STD = 0.1

kernel_name = "diffattn_retention_hybrid_step"


def rms_norm(x, g):
    x32 = x.astype(jnp.float32)
    y = x32 * lax.rsqrt(jnp.mean(x32 * x32, axis=-1, keepdims=True) + EPS)
    return (y * g.astype(jnp.float32)).astype(x.dtype)


def rms_unit(x32):
    return x32 * lax.rsqrt(jnp.mean(x32 * x32, axis=-1, keepdims=True) + EPS)


def sqrelu_mlp(h, w_up, w_down):
    u = jax.nn.relu(h @ w_up)
    return (u * u) @ w_down


def alibi_slopes(n):
    return 2.0 ** (-8.0 * jnp.arange(1, n + 1, dtype=jnp.float32) / n)


def retention_log_decay(n):
    return jnp.log1p(-(2.0 ** (-5.0 - jnp.arange(n, dtype=jnp.float32))))


def lambda_init_fn(layer_idx):
    return 0.8 - 0.6 * math.exp(-0.3 * layer_idx)


def diff_qkv(h, w_in, q_g, k_g):
    B, L, _ = h.shape
    qkv = h @ w_in
    q, k, v = jnp.split(qkv, [DIFF_Q_DIM, 2 * DIFF_Q_DIM], axis=-1)
    q = rms_norm(q.reshape(B, L, N_HEADS_DIFF, 2, DIFF_HEAD_DIM), q_g) * (DIFF_HEAD_DIM ** -0.5)
    k = rms_norm(k.reshape(B, L, N_HEADS_DIFF, 2, DIFF_HEAD_DIM), k_g)
    v = v.reshape(B, L, N_HEADS_DIFF, DIFF_V_DIM)
    return q, k, v


def diff_combine(q, k, v, q_pos, k_pos, lam):
    s = jnp.einsum('bqhcd,bkhcd->bchqk', q, k).astype(jnp.float32)
    dist = (q_pos[:, None] - k_pos[None, :]).astype(jnp.float32)
    slopes = alibi_slopes(N_HEADS_DIFF)
    s = jnp.where(dist >= 0, s - slopes[:, None, None] * dist, -jnp.inf)
    p = jax.nn.softmax(s, axis=-1)
    a = p[:, 0] - lam * p[:, 1]
    return jnp.einsum('bhqk,bkhe->bqhe', a.astype(v.dtype), v)


def diff_attn_prompt(q, k, v, lam):
    B, L = q.shape[:2]
    nb = L // Q_BLOCK
    qb = q.reshape(B, nb, Q_BLOCK, N_HEADS_DIFF, 2, DIFF_HEAD_DIM).transpose(1, 0, 2, 3, 4, 5)
    starts = jnp.arange(nb, dtype=jnp.int32) * Q_BLOCK
    k_pos = jnp.arange(L, dtype=jnp.int32)

    def block(args):
        qi, s0 = args
        return diff_combine(qi, k, v, s0 + jnp.arange(Q_BLOCK, dtype=jnp.int32), k_pos, lam)

    o = lax.map(block, (qb, starts))
    return o.transpose(1, 0, 2, 3, 4).reshape(B, L, N_HEADS_DIFF, DIFF_V_DIM)


def diff_attn_sample(q, k, v, cache_k, cache_v, page_table, layer, lam):
    Ln = q.shape[1]
    past = page_table.shape[1] * PAGE_SIZE
    q_pos = past + jnp.arange(Ln, dtype=jnp.int32)
    k_pos = jnp.arange(past + Ln, dtype=jnp.int32)

    def one(args):
        qb, kb, vb, pt = args
        kp = cache_k[layer, pt].reshape(past, N_HEADS_DIFF, 2, DIFF_HEAD_DIM).astype(kb.dtype)
        vp = cache_v[layer, pt].reshape(past, N_HEADS_DIFF, DIFF_V_DIM).astype(vb.dtype)
        k_all = jnp.concatenate([kp, kb], axis=0)[None]
        v_all = jnp.concatenate([vp, vb], axis=0)[None]
        return diff_combine(qb[None], k_all, v_all, q_pos, k_pos, lam)[0]

    return lax.map(one, (q, k, v, page_table))


def diff_out(o, subln_g, w_out, lambda_init):
    B, L = o.shape[:2]
    o32 = rms_unit(o.astype(jnp.float32)) * subln_g.astype(jnp.float32) * (1.0 - lambda_init)
    return o32.astype(o.dtype).reshape(B, L, DIFF_VALS_DIM) @ w_out


def ret_proj(h, w_in):
    B, L, _ = h.shape
    z = h @ w_in
    q, k, v, g = jnp.split(z, [RET_QK_TOTAL, 2 * RET_QK_TOTAL, 2 * RET_QK_TOTAL + RET_V_TOTAL], axis=-1)
    q = q.reshape(B, L, N_HEADS_RET, RET_QK_DIM)
    k = k.reshape(B, L, N_HEADS_RET, RET_QK_DIM) * (RET_QK_DIM ** -0.5)
    v = v.reshape(B, L, N_HEADS_RET, RET_V_DIM)
    return q, k, v, g


def retention_chunk(q, k, v, S, log_gamma):
    q = q.astype(jnp.float32); k = k.astype(jnp.float32); v = v.astype(jnp.float32)
    L = q.shape[1]
    idx = jnp.arange(L, dtype=jnp.float32)
    diff = idx[:, None] - idx[None, :]
    D = jnp.where(diff >= 0, jnp.exp(log_gamma[:, None, None] * jnp.maximum(diff, 0.0)), 0.0)
    qk = jnp.einsum('bihd,bjhd->bhij', q, k) * D
    inner = jnp.einsum('bhij,bjhe->bihe', qk, v)
    q_dec = jnp.exp(log_gamma[None, :] * (idx[:, None] + 1.0))
    cross = jnp.einsum('bihd,bhde->bihe', q * q_dec[None, :, :, None], S)
    k_dec = jnp.exp(log_gamma[None, :] * (L - 1.0 - idx)[:, None])
    S_new = jnp.exp(log_gamma * L)[None, :, None, None] * S + \
        jnp.einsum('bjhd,bjhe->bhde', k * k_dec[None, :, :, None], v)
    return inner + cross, S_new


def retention_prompt(q, k, v):
    B, L = q.shape[:2]
    nc = L // RET_CHUNK
    lg = retention_log_decay(N_HEADS_RET)
    qc = q.reshape(B, nc, RET_CHUNK, N_HEADS_RET, RET_QK_DIM).swapaxes(0, 1)
    kc = k.reshape(B, nc, RET_CHUNK, N_HEADS_RET, RET_QK_DIM).swapaxes(0, 1)
    vc = v.reshape(B, nc, RET_CHUNK, N_HEADS_RET, RET_V_DIM).swapaxes(0, 1)

    def step(S, xs):
        qi, ki, vi = xs
        o, S = retention_chunk(qi, ki, vi, S, lg)
        return S, o

    S0 = jnp.zeros((B, N_HEADS_RET, RET_QK_DIM, RET_V_DIM), jnp.float32)
    S_fin, oc = lax.scan(step, S0, (qc, kc, vc))
    return oc.swapaxes(0, 1).reshape(B, L, N_HEADS_RET, RET_V_DIM), S_fin


def ret_out(o32, g, w_out):
    B, L = o32.shape[:2]
    y = rms_unit(o32).reshape(B, L, RET_V_TOTAL).astype(g.dtype)
    return (jax.nn.silu(g) * y) @ w_out


def setup_inputs(seed: int = 0) -> dict:
    key = jax.random.key(seed)
    ks = jax.random.split(key, 24)
    f32 = jnp.float32
    n_pages = PAST_LEN // PAGE_SIZE
    n_used = DEC_BATCH * n_pages
    n_pool = (n_used * 5) // 4
    nrm = lambda k, s, sc: jax.random.normal(k, s, f32) * sc
    x_prompt = nrm(ks[0], (BATCH, SEQ, D_MODEL), 1.0)
    x_sample = nrm(ks[1], (DEC_BATCH, DEC_SEQ, D_MODEL), 1.0)
    cache_k = nrm(ks[2], (N_ATTN_LAYERS, n_pool, PAGE_SIZE, N_HEADS_DIFF, 2 * DIFF_HEAD_DIM), 1.0)
    cache_v = nrm(ks[3], (N_ATTN_LAYERS, n_pool, PAGE_SIZE, N_HEADS_DIFF, DIFF_V_DIM), 1.0)
    state_ret = nrm(ks[4], (N_RET_LAYERS, DEC_BATCH, N_HEADS_RET, RET_QK_DIM, RET_V_DIM), 1.0)
    page_table = jax.random.permutation(ks[5], n_pool)[:n_used].reshape(DEC_BATCH, n_pages).astype(jnp.int32)
    return {
        "x_prompt": x_prompt,
        "x_sample": x_sample,
        "cache_k": cache_k,
        "cache_v": cache_v,
        "state_ret": state_ret,
        "page_table": page_table,
        "norm_mix": 1.0 + nrm(ks[6], (DEPTH, D_MODEL), 0.02),
        "norm_ffn": 1.0 + nrm(ks[7], (DEPTH, D_MODEL), 0.02),
        "w_attn_in": nrm(ks[8], (N_ATTN_LAYERS, D_MODEL, 2 * DIFF_Q_DIM + DIFF_VALS_DIM), D_MODEL ** -0.5),
        "q_norm_g": 1.0 + nrm(ks[9], (N_ATTN_LAYERS, DIFF_HEAD_DIM), 0.02),
        "k_norm_g": 1.0 + nrm(ks[10], (N_ATTN_LAYERS, DIFF_HEAD_DIM), 0.02),
        "lambda_q1": nrm(ks[11], (N_ATTN_LAYERS, DIFF_HEAD_DIM), LAMBDA_INIT_STD),
        "lambda_k1": nrm(ks[12], (N_ATTN_LAYERS, DIFF_HEAD_DIM), LAMBDA_INIT_STD),
        "lambda_q2": nrm(ks[13], (N_ATTN_LAYERS, DIFF_HEAD_DIM), LAMBDA_INIT_STD),
        "lambda_k2": nrm(ks[14], (N_ATTN_LAYERS, DIFF_HEAD_DIM), LAMBDA_INIT_STD),
        "subln_g": 1.0 + nrm(ks[15], (N_ATTN_LAYERS, DIFF_V_DIM), 0.02),
        "w_attn_out": nrm(ks[16], (N_ATTN_LAYERS, DIFF_VALS_DIM, D_MODEL), DIFF_VALS_DIM ** -0.5),
        "w_ret_in": nrm(ks[17], (N_RET_LAYERS, D_MODEL, 2 * RET_QK_TOTAL + 2 * RET_V_TOTAL), D_MODEL ** -0.5),
        "w_ret_out": nrm(ks[18], (N_RET_LAYERS, RET_V_TOTAL, D_MODEL), RET_V_TOTAL ** -0.5),
        "w_up": nrm(ks[19], (DEPTH, D_MODEL, D_FF), D_MODEL ** -0.5),
        "w_down": nrm(ks[20], (DEPTH, D_FF, D_MODEL), D_FF ** -0.5),
    }


def reference(x_prompt, x_sample, cache_k, cache_v, state_ret, page_table,
              norm_mix, norm_ffn, w_attn_in, q_norm_g, k_norm_g,
              lambda_q1, lambda_k1, lambda_q2, lambda_k2, subln_g, w_attn_out,
              w_ret_in, w_ret_out, w_up, w_down):
    xp, xs = x_prompt, x_sample
    kp_rows, vp_rows, ks_rows, vs_rows, sp_list, ss_list = [], [], [], [], [], []
    for i in range(DEPTH):
        hp = rms_norm(xp, norm_mix[i])
        hs = rms_norm(xs, norm_mix[i])
        if i % 2 == 0:
            a = i // 2
            lam_init = lambda_init_fn(i)
            f32 = jnp.float32
            lam = (jnp.exp(jnp.sum(lambda_q1[a].astype(f32) * lambda_k1[a].astype(f32)))
                   - jnp.exp(jnp.sum(lambda_q2[a].astype(f32) * lambda_k2[a].astype(f32))) + lam_init)
            qp, kp, vp = diff_qkv(hp, w_attn_in[a], q_norm_g[a], k_norm_g[a])
            qs, ks_, vs = diff_qkv(hs, w_attn_in[a], q_norm_g[a], k_norm_g[a])
            op = diff_attn_prompt(qp, kp, vp, lam)
            os_ = diff_attn_sample(qs, ks_, vs, cache_k, cache_v, page_table, a, lam)
            xp = xp + diff_out(op, subln_g[a], w_attn_out[a], lam_init)
            xs = xs + diff_out(os_, subln_g[a], w_attn_out[a], lam_init)
            kp_rows.append(kp.reshape(kp.shape[0], kp.shape[1], N_HEADS_DIFF, 2 * DIFF_HEAD_DIM))
            vp_rows.append(vp)
            ks_rows.append(ks_.reshape(ks_.shape[0], ks_.shape[1], N_HEADS_DIFF, 2 * DIFF_HEAD_DIM))
            vs_rows.append(vs)
        else:
            r = i // 2
            lg = retention_log_decay(N_HEADS_RET)
            qp, kp, vp, gp = ret_proj(hp, w_ret_in[r])
            qs, ks_, vs, gs = ret_proj(hs, w_ret_in[r])
            op, sp = retention_prompt(qp, kp, vp)
            os_, ss = retention_chunk(qs, ks_, vs, state_ret[r].astype(jnp.float32), lg)
            xp = xp + ret_out(op, gp, w_ret_out[r])
            xs = xs + ret_out(os_, gs, w_ret_out[r])
            sp_list.append(sp.astype(state_ret.dtype))
            ss_list.append(ss.astype(state_ret.dtype))
        xp = xp + sqrelu_mlp(rms_norm(xp, norm_ffn[i]), w_up[i], w_down[i])
        xs = xs + sqrelu_mlp(rms_norm(xs, norm_ffn[i]), w_up[i], w_down[i])
    k_prompt = jnp.stack(kp_rows)
    v_prompt = jnp.stack(vp_rows)
    k_sample = jnp.stack(ks_rows)
    v_sample = jnp.stack(vs_rows)
    s_prompt = jnp.stack(sp_list)
    s_sample = jnp.stack(ss_list)
    return (xp, xs, k_prompt, v_prompt, k_sample, v_sample, s_prompt, s_sample)
```

```python
import functools
import math

import jax
import jax.numpy as jnp
from jax import lax
from jax.experimental import pallas as pl
from jax.experimental.pallas import tpu as pltpu

F32 = jnp.float32
BF16 = jnp.bfloat16
EPS = 1e-6
NEG = -1e30
LANES = 128
VMEM_LIMIT = 56 * 1024 * 1024
PAGES_PER_STEP = 8
RET_CHUNK = 128
PAGE_SIZE = 128


def _params(sem):
    return pltpu.CompilerParams(dimension_semantics=sem, vmem_limit_bytes=VMEM_LIMIT)


def _resident(shape):
    nd = len(shape)
    return pl.BlockSpec(shape, lambda *_: (0,) * nd, pipeline_mode=pl.Buffered(1))


def _rmsnorm_bf16(x, g):
    return (x * lax.rsqrt(jnp.mean(x * x, axis=-1, keepdims=True) + EPS) * g).astype(BF16)


def _dot(a, b):
    return jnp.dot(a, b, preferred_element_type=F32)


def _dot_nt(a, b):
    return lax.dot_general(a, b, (((1,), (1,)), ((), ())), preferred_element_type=F32)


def _qkv_kernel(x_ref, g_ref, w_ref, qg_ref, kg_ref, qb_ref, kf_ref, kb_ref, vf_ref, vb_ref,
                *, d, half, q_scale):
    h = _rmsnorm_bf16(x_ref[...], g_ref[...])
    tm = h.shape[0]
    lo = lax.broadcasted_iota(jnp.int32, (tm, LANES), 1) < half
    inv = 1.0 / half

    def half_norm(a, gain):
        sq = a * a
        s_lo = jnp.sum(jnp.where(lo, sq, 0.0), axis=-1, keepdims=True)
        s_hi = jnp.sum(jnp.where(lo, 0.0, sq), axis=-1, keepdims=True)
        r = jnp.where(lo, lax.rsqrt(s_lo * inv + EPS), lax.rsqrt(s_hi * inv + EPS))
        return a * r * gain

    q = _dot(h, w_ref[:, 0:d])
    for c in range(d // LANES):
        sl = slice(c * LANES, (c + 1) * LANES)
        qb_ref[:, sl] = (half_norm(q[:, sl], qg_ref[:, sl]) * q_scale).astype(BF16)
    k = _dot(h, w_ref[:, d:2 * d])
    for c in range(d // LANES):
        sl = slice(c * LANES, (c + 1) * LANES)
        kn = half_norm(k[:, sl], kg_ref[:, sl])
        kf_ref[:, sl] = kn
        kb_ref[:, sl] = kn.astype(BF16)
    v = _dot(h, w_ref[:, 2 * d:3 * d])
    vf_ref[...] = v
    vb_ref[...] = v.astype(BF16)


def _qkv(x, g, w, qg, kg, *, tm, half):
    m, d = x.shape
    row = lambda i: (i, 0)
    outs = [jax.ShapeDtypeStruct((m, d), t) for t in (BF16, F32, BF16, F32, BF16)]
    return pl.pallas_call(
        functools.partial(_qkv_kernel, d=d, half=half, q_scale=half ** -0.5),
        grid=(m // tm,),
        in_specs=[pl.BlockSpec((tm, d), row), _resident((1, d)), _resident(w.shape),
                  _resident((1, d)), _resident((1, d))],
        out_specs=[pl.BlockSpec((tm, d), row)] * 5,
        out_shape=outs,
        compiler_params=_params(("parallel",)),
        name="qkv_proj",
    )(x, g, w, qg, kg)


def _lambda_11(l_ref, lam_init):
    a = jnp.sum(l_ref[0:1, :] * l_ref[1:2, :], axis=-1, keepdims=True)
    b = jnp.sum(l_ref[2:3, :] * l_ref[3:4, :], axis=-1, keepdims=True)
    return jnp.exp(a) - jnp.exp(b) + lam_init


def _subln_bf16(o, sg, lam_init):
    r = lax.rsqrt(jnp.mean(o * o, axis=-1, keepdims=True) + EPS)
    return (o * r * sg * (1.0 - lam_init)).astype(BF16)


def _softmax_step(s, v, m, l, acc):
    m_new = jnp.maximum(m, jnp.max(s, axis=-1, keepdims=True))
    alpha = jnp.exp(m - m_new)
    p = jnp.exp(s - m_new)
    l_new = alpha * l + jnp.sum(p, axis=-1, keepdims=True)
    acc_new = alpha * acc + _dot(p.astype(BF16), v)
    return m_new, l_new, acc_new


def _attn_prompt_kernel(slope_ref, lam_ref, q_ref, k_ref, v_ref, sg_ref, o_ref,
                        *, seq, tq, tk, half, lam_init):
    head = pl.program_id(1)
    slope = slope_ref[head]
    lam = _lambda_11(lam_ref, lam_init)
    lo = lax.broadcasted_iota(jnp.int32, (tq, LANES), 1) < half
    col = lax.broadcasted_iota(jnp.int32, (1, tk), 1)
    row2 = lax.broadcasted_iota(jnp.int32, (2 * tq, tk), 0) & (tq - 1)
    col2 = lax.broadcasted_iota(jnp.int32, (2 * tq, tk), 1)
    zero = jnp.zeros((), BF16)

    for qi in range(seq // tq):
        q0 = qi * tq
        q = q_ref[q0:q0 + tq, :]
        q2 = jnp.concatenate([jnp.where(lo, q, zero), jnp.where(lo, zero, q)], axis=0)

        def tile(k0, carry, masked):
            k = k_ref[pl.ds(k0, tk), :]
            v = v_ref[pl.ds(k0, tk), :]
            bias = slope * (col + (k0 - q0)).astype(F32)
            s = _dot_nt(q2, k) + bias
            if masked:
                s = jnp.where(col2 + (k0 - q0) <= row2, s, NEG)
            return _softmax_step(s, v, *carry)

        carry = (jnp.full((2 * tq, 1), NEG, F32), jnp.zeros((2 * tq, 1), F32),
                 jnp.zeros((2 * tq, LANES), F32))
        n_full = q0 // tk
        if n_full > 0:
            carry = lax.fori_loop(
                0, n_full, lambda i, c: tile(pl.multiple_of(i * tk, tk), c, False), carry)
        for kd in range(tq // tk):
            carry = tile(q0 + kd * tk, carry, True)
        _, l, acc = carry
        t = acc / l
        o = t[:tq] - lam * t[tq:]
        o_ref[q0:q0 + tq, :] = _subln_bf16(o, sg_ref[...], lam_init)


def _attn_prompt(slopes, lam4, qb, kb, vb, sg, *, batch, seq, heads, half, lam_init, tq, tk):
    m, d = qb.shape
    blk = pl.BlockSpec((seq, LANES), lambda b, h: (b, h))
    return pl.pallas_call(
        functools.partial(_attn_prompt_kernel, seq=seq, tq=tq, tk=tk, half=half,
                          lam_init=lam_init),
        grid=(batch, heads),
        in_specs=[pl.BlockSpec(memory_space=pltpu.SMEM), _resident(lam4.shape),
                  blk, blk, blk, _resident(sg.shape)],
        out_specs=blk,
        out_shape=jax.ShapeDtypeStruct((m, d), BF16),
        compiler_params=_params(("parallel", "parallel")),
        name="attn_prompt",
    )(slopes, lam4, qb, kb, vb, sg)


def _attn_sample_kernel(pt_ref, slope_ref, lam_ref, q_ref, kn_ref, vn_ref, sg_ref, *rest,
                        pages, heads, dec_seq, half, past, lam_init):
    k_refs = rest[:pages]
    v_refs = rest[pages:2 * pages]
    o_ref, m_scr, l_scr, acc_scr = rest[2 * pages:]
    j = pl.program_id(1)
    rows = q_ref.shape[0]
    span = pages * PAGE_SIZE

    @pl.when(j == 0)
    def _():
        m_scr[...] = jnp.full(m_scr.shape, NEG, F32)
        l_scr[...] = jnp.zeros(l_scr.shape, F32)
        acc_scr[...] = jnp.zeros(acc_scr.shape, F32)

    r_id = lax.broadcasted_iota(jnp.int32, (rows, LANES), 0)
    lane = lax.broadcasted_iota(jnp.int32, (rows, LANES), 1)
    keep = ((r_id < dec_seq) & (lane < half)) | \
           ((r_id >= dec_seq) & (r_id < 2 * dec_seq) & (lane >= half))
    zero = jnp.zeros((), BF16)

    def q_head(h):
        return jnp.where(keep, q_ref[:, h * LANES:(h + 1) * LANES], zero)

    def update(h, s, v):
        m, l, acc = _softmax_step(s, v, m_scr[h], l_scr[h], acc_scr[h])
        m_scr[h] = m
        l_scr[h] = l
        acc_scr[h] = acc

    col = lax.broadcasted_iota(jnp.int32, (1, span), 1)
    for h in range(heads):
        k = jnp.concatenate(
            [r[pl.ds(h, PAGE_SIZE, stride=heads), :].astype(BF16) for r in k_refs], axis=0)
        v = jnp.concatenate(
            [r[pl.ds(h, PAGE_SIZE, stride=heads), :].astype(BF16) for r in v_refs], axis=0)
        bias = slope_ref[h] * (col + (j * span - past)).astype(F32)
        update(h, _dot_nt(q_head(h), k) + bias, v)

    @pl.when(j == pl.num_programs(1) - 1)
    def _():
        lam = _lambda_11(lam_ref, lam_init)
        t_id = lax.broadcasted_iota(jnp.int32, (rows, PAGE_SIZE), 1)
        q_id = lax.broadcasted_iota(jnp.int32, (rows, PAGE_SIZE), 0) & (dec_seq - 1)
        valid = t_id <= q_id
        for h in range(heads):
            sl = slice(h * LANES, (h + 1) * LANES)
            s = _dot_nt(q_head(h), kn_ref[:, sl]) + slope_ref[h] * t_id.astype(F32)
            update(h, jnp.where(valid, s, NEG), vn_ref[:, sl])
            t = acc_scr[h] / l_scr[h]
            o = t - lam * pltpu.roll(t, rows - dec_seq, axis=0)
            o_ref[:, sl] = _subln_bf16(o, sg_ref[...], lam_init)


def _attn_sample(page_table, slopes, lam4, q16, kn, vn, sg, ck, cv, *, layer, heads, dec_seq,
                 half, lam_init):
    nb, rows, d = q16.shape
    n_pages = page_table.shape[1]
    pages = PAGES_PER_STEP
    past = n_pages * PAGE_SIZE
    per_b = lambda b, j, pt: (b, 0, 0)
    const2 = lambda b, j, pt: (0, 0)
    page_specs = [
        pl.BlockSpec((None, None, PAGE_SIZE * heads, LANES),
                     functools.partial(lambda b, j, pt, i: (layer, pt[b, j * pages + i], 0, 0), i=i))
        for i in range(pages)]
    grid_spec = pltpu.PrefetchScalarGridSpec(
        num_scalar_prefetch=1,
        grid=(nb, n_pages // pages),
        in_specs=[pl.BlockSpec(memory_space=pltpu.SMEM),
                  pl.BlockSpec(lam4.shape, const2),
                  pl.BlockSpec((None, rows, d), per_b),
                  pl.BlockSpec((None, PAGE_SIZE, d), per_b),
                  pl.BlockSpec((None, PAGE_SIZE, d), per_b),
                  pl.BlockSpec(sg.shape, const2)] + page_specs + page_specs,
        out_specs=pl.BlockSpec((None, rows, d), per_b),
        scratch_shapes=[pltpu.VMEM((heads, rows, 1), F32), pltpu.VMEM((heads, rows, 1), F32),
                        pltpu.VMEM((heads, rows, LANES), F32)],
    )
    return pl.pallas_call(
        functools.partial(_attn_sample_kernel, pages=pages, heads=heads, dec_seq=dec_seq,
                          half=half, past=past, lam_init=lam_init),
        grid_spec=grid_spec,
        out_shape=jax.ShapeDtypeStruct((nb, rows, d), BF16),
        compiler_params=_params(("parallel", "arbitrary")),
        name="attn_sample",
    )(page_table, slopes, lam4, q16, kn, vn, sg, *([ck] * pages), *([cv] * pages))


def _proj_mlp_kernel(x_ref, a_ref, wo_ref, g_ref, wu_ref, wd_ref, o_ref, *, ff_chunk):
    x1 = x_ref[...] + _dot(a_ref[...], wo_ref[...])
    h = _rmsnorm_bf16(x1, g_ref[...])
    acc = x1
    for c in range(wu_ref.shape[1] // ff_chunk):
        sl = slice(c * ff_chunk, (c + 1) * ff_chunk)
        u = jnp.maximum(_dot(h, wu_ref[:, sl]), 0.0)
        acc = acc + _dot((u * u).astype(BF16), wd_ref[sl, :])
    o_ref[...] = acc


def _proj_mlp(x, a, wo, g, wu, wd, *, tm):
    m, d = x.shape
    row = lambda i: (i, 0)
    return pl.pallas_call(
        functools.partial(_proj_mlp_kernel, ff_chunk=min(1024, wu.shape[1])),
        grid=(m // tm,),
        in_specs=[pl.BlockSpec((tm, d), row), pl.BlockSpec((tm, a.shape[1]), row),
                  _resident(wo.shape), _resident((1, d)), _resident(wu.shape), _resident(wd.shape)],
        out_specs=pl.BlockSpec((tm, d), row),
        out_shape=jax.ShapeDtypeStruct((m, d), F32),
        compiler_params=_params(("parallel",)),
        name="proj_mlp",
    )(x, a, wo, g, wu, wd)


def _ret_in_kernel(x_ref, g_ref, w_ref, z_ref, *, n_chunk):
    h = _rmsnorm_bf16(x_ref[...], g_ref[...])
    for c in range(w_ref.shape[1] // n_chunk):
        sl = slice(c * n_chunk, (c + 1) * n_chunk)
        z_ref[:, sl] = _dot(h, w_ref[:, sl]).astype(BF16)


def _ret_in(x, g, w, *, tm):
    m, d = x.shape
    n = w.shape[1]
    row = lambda i: (i, 0)
    return pl.pallas_call(
        functools.partial(_ret_in_kernel, n_chunk=min(1024, n)),
        grid=(m // tm,),
        in_specs=[pl.BlockSpec((tm, d), row), _resident((1, d)), _resident(w.shape)],
        out_specs=pl.BlockSpec((tm, n), row),
        out_shape=jax.ShapeDtypeStruct((m, n), BF16),
        compiler_params=_params(("parallel",)),
        name="ret_in_proj",
    )(x, g, w)


def _retention_kernel(lg_ref, q_ref, k_ref, v_ref, g_ref, *rest, n_chunks, chunk_len, k_scale,
                      has_state):
    if has_state:
        s0_ref, y_ref, s_ref = rest
    else:
        y_ref, s_ref = rest
    c = RET_CHUNK
    lg = lg_ref[pl.program_id(1)]
    idx = lax.broadcasted_iota(jnp.int32, (c, 1), 0).astype(F32)
    diff = (lax.broadcasted_iota(jnp.int32, (c, c), 0)
            - lax.broadcasted_iota(jnp.int32, (c, c), 1)).astype(F32)
    dmat = jnp.where(diff >= 0, jnp.exp(lg * jnp.maximum(diff, 0.0)), 0.0) * k_scale
    q_dec = jnp.exp(lg * (idx + 1.0))
    k_dec = jnp.exp(lg * (chunk_len - 1.0 - idx)) * k_scale
    s_dec = jnp.exp(jnp.full((1, 1), lg * chunk_len, F32))

    if has_state:
        s_ref[...] = s0_ref[...]
    else:
        s_ref[...] = jnp.zeros(s_ref.shape, F32)

    def chunk(i, _):
        r0 = pl.multiple_of(i * c, c)
        q = q_ref[pl.ds(r0, c), :]
        k = k_ref[pl.ds(r0, c), :]
        v = v_ref[pl.ds(r0, c), :]
        state = s_ref[...]
        qk = _dot_nt(q, k) * dmat
        o = _dot(qk.astype(BF16), v) + q_dec * _dot(q, state.astype(BF16))
        kd_t = (k.astype(F32) * k_dec).T.astype(BF16)
        s_ref[...] = s_dec * state + _dot(kd_t, v)
        y = o * lax.rsqrt(jnp.mean(o * o, axis=-1, keepdims=True) + EPS)
        g = g_ref[pl.ds(r0, c), :].astype(F32)
        y_ref[pl.ds(r0, c), :] = (g * jax.nn.sigmoid(g) * y).astype(BF16)
        return 0

    lax.fori_loop(0, n_chunks, chunk, 0)


def _retention(lg, z, state, *, n_seq, seq, heads, dk, dv, chunk_len):
    m = z.shape[0]
    qk_blocks = heads * dk // dk
    v_off = 2 * heads * dk // dv
    in_specs = [pl.BlockSpec(memory_space=pltpu.SMEM),
                pl.BlockSpec((seq, dk), lambda b, h: (b, h)),
                pl.BlockSpec((seq, dk), lambda b, h: (b, qk_blocks + h)),
                pl.BlockSpec((seq, dv), lambda b, h: (b, v_off + h)),
                pl.BlockSpec((seq, dv), lambda b, h: (b, v_off + heads + h))]
    args = [lg, z, z, z, z]
    st_spec = pl.BlockSpec((None, None, dk, dv), lambda b, h: (b, h, 0, 0))
    if state is not None:
        in_specs.append(st_spec)
        args.append(state)
    return pl.pallas_call(
        functools.partial(_retention_kernel, n_chunks=seq // RET_CHUNK, chunk_len=chunk_len,
                          k_scale=dk ** -0.5, has_state=state is not None),
        grid=(n_seq, heads),
        in_specs=in_specs,
        out_specs=[pl.BlockSpec((seq, dv), lambda b, h: (b, h)), st_spec],
        out_shape=[jax.ShapeDtypeStruct((m, heads * dv), BF16),
                   jax.ShapeDtypeStruct((n_seq, heads, dk, dv), F32)],
        compiler_params=_params(("parallel", "parallel")),
        name="retention",
    )(*args)


def _row_tile(m):
    return min(m, 512)


def kernel(x_prompt, x_sample, cache_k, cache_v, state_ret, page_table, norm_mix, norm_ffn,
           w_attn_in, q_norm_g, k_norm_g, lambda_q1, lambda_k1, lambda_q2, lambda_k2, subln_g,
           w_attn_out, w_ret_in, w_ret_out, w_up, w_down):
    batch, seq, d = x_prompt.shape
    dec_batch, dec_seq, _ = x_sample.shape
    depth = norm_mix.shape[0]
    half = q_norm_g.shape[1]
    heads_a = d // (2 * half)
    heads_r, dk, dv = state_ret.shape[2:]
    n_pool = cache_k.shape[1]
    assert 2 * half == LANES and cache_k.shape[2] == PAGE_SIZE and dec_seq & (dec_seq - 1) == 0

    xp = x_prompt.reshape(batch * seq, d)
    xs = x_sample.reshape(dec_batch * dec_seq, d)
    tp, ts = _row_tile(xp.shape[0]), _row_tile(xs.shape[0])
    ck = cache_k.reshape(cache_k.shape[0], n_pool, PAGE_SIZE * heads_a, LANES)
    cv = cache_v.reshape(cache_v.shape[0], n_pool, PAGE_SIZE * heads_a, LANES)
    slopes = 2.0 ** (-8.0 * jnp.arange(1, heads_a + 1, dtype=F32) / heads_a)
    log_gamma = jnp.log1p(-(2.0 ** (-5.0 - jnp.arange(heads_r, dtype=F32))))
    q_rows = 4 * dec_seq

    kp_l, vp_l, ks_l, vs_l, sp_l, ss_l = [], [], [], [], [], []
    for i in range(depth):
        g_mix = norm_mix[i].reshape(1, d)
        g_ffn = norm_ffn[i].reshape(1, d)
        wu, wd = w_up[i].astype(BF16), w_down[i].astype(BF16)
        if i % 2 == 0:
            a = i // 2
            lam_init = 0.8 - 0.6 * math.exp(-0.3 * i)
            w_in, wo = w_attn_in[a].astype(BF16), w_attn_out[a].astype(BF16)
            qg = jnp.tile(q_norm_g[a], d // half).reshape(1, d)
            kg = jnp.tile(k_norm_g[a], d // half).reshape(1, d)
            sg = subln_g[a].reshape(1, LANES)
            lam4 = jnp.stack([lambda_q1[a], lambda_k1[a], lambda_q2[a], lambda_k2[a]])
            qb, kf, kb, vf, vb = _qkv(xp, g_mix, w_in, qg, kg, tm=tp, half=half)
            qbs, kfs, kbs, vfs, vbs = _qkv(xs, g_mix, w_in, qg, kg, tm=ts, half=half)
            op = _attn_prompt(slopes, lam4, qb, kb, vb, sg, batch=batch, seq=seq, heads=heads_a,
                              half=half, lam_init=lam_init, tq=min(512, seq), tk=min(512, seq))
            q3 = qbs.reshape(dec_batch, dec_seq, d)
            q16 = jnp.concatenate(
                [q3, q3, jnp.zeros((dec_batch, q_rows - 2 * dec_seq, d), BF16)], axis=1)
            pad = ((0, 0), (0, PAGE_SIZE - dec_seq), (0, 0))
            kn = jnp.pad(kbs.reshape(dec_batch, dec_seq, d), pad)
            vn = jnp.pad(vbs.reshape(dec_batch, dec_seq, d), pad)
            os_ = _attn_sample(page_table, slopes, lam4, q16, kn, vn, sg, ck, cv, layer=a,
                               heads=heads_a, dec_seq=dec_seq, half=half, lam_init=lam_init)
            os_ = os_[:, :dec_seq].reshape(dec_batch * dec_seq, d)
            kp_l.append(kf.reshape(batch, seq, heads_a, LANES))
            vp_l.append(vf.reshape(batch, seq, heads_a, LANES))
            ks_l.append(kfs.reshape(dec_batch, dec_seq, heads_a, LANES))
            vs_l.append(vfs.reshape(dec_batch, dec_seq, heads_a, LANES))
        else:
            r = i // 2
            w_in, wo = w_ret_in[r].astype(BF16), w_ret_out[r].astype(BF16)
            zp = _ret_in(xp, g_mix, w_in, tm=tp)
            zs = _ret_in(xs, g_mix, w_in, tm=ts)
            op, sp = _retention(log_gamma, zp, None, n_seq=batch, seq=seq, heads=heads_r,
                                dk=dk, dv=dv, chunk_len=RET_CHUNK)
            zs = jnp.pad(zs.reshape(dec_batch, dec_seq, -1),
                         ((0, 0), (0, RET_CHUNK - dec_seq), (0, 0)))
            os_, ss = _retention(log_gamma, zs.reshape(dec_batch * RET_CHUNK, -1), state_ret[r],
                                 n_seq=dec_batch, seq=RET_CHUNK, heads=heads_r, dk=dk, dv=dv,
                                 chunk_len=dec_seq)
            os_ = os_.reshape(dec_batch, RET_CHUNK, -1)[:, :dec_seq].reshape(dec_batch * dec_seq, -1)
            sp_l.append(sp)
            ss_l.append(ss)
        xp = _proj_mlp(xp, op, wo, g_ffn, wu, wd, tm=tp)
        xs = _proj_mlp(xs, os_, wo, g_ffn, wu, wd, tm=ts)

    return (xp.reshape(batch, seq, d), xs.reshape(dec_batch, dec_seq, d),
            jnp.stack(kp_l), jnp.stack(vp_l), jnp.stack(ks_l), jnp.stack(vs_l),
            jnp.stack(sp_l), jnp.stack(ss_l))
```

```python
import functools
import math

import jax
import jax.numpy as jnp
from jax import lax
from jax.experimental import pallas as pl
from jax.experimental.pallas import tpu as pltpu

F32 = jnp.float32
BF16 = jnp.bfloat16
EPS = 1e-6
NEG = -1e30
LANES = 128
VMEM_LIMIT = 56 * 1024 * 1024
PAGES_PER_STEP = 8
RET_CHUNK = 128
PAGE_SIZE = 128
SAMPLE_ROWS = 16
LOG2E = math.log2(math.e)


def _params(sem):
    return pltpu.CompilerParams(dimension_semantics=sem, vmem_limit_bytes=VMEM_LIMIT)


def _resident(shape):
    nd = len(shape)
    return pl.BlockSpec(shape, lambda *_: (0,) * nd, pipeline_mode=pl.Buffered(1))


def _rmsnorm_bf16(x, g):
    return (x * lax.rsqrt(jnp.mean(x * x, axis=-1, keepdims=True) + EPS) * g).astype(BF16)


def _dot(a, b):
    return jnp.dot(a, b, preferred_element_type=F32)


def _dot_nt(a, b):
    return lax.dot_general(a, b, (((1,), (1,)), ((), ())), preferred_element_type=F32)


def _qkv_kernel(x_ref, g_ref, w_ref, qg_ref, kg_ref, qb_ref, kf_ref, kb_ref, vf_ref, vb_ref,
                *, d, half, q_scale):
    h = _rmsnorm_bf16(x_ref[...], g_ref[...])
    tm = h.shape[0]
    lo = lax.broadcasted_iota(jnp.int32, (tm, LANES), 1) < half
    inv = 1.0 / half

    def half_norm(a, gain):
        sq = a * a
        s_lo = jnp.sum(jnp.where(lo, sq, 0.0), axis=-1, keepdims=True)
        s_hi = jnp.sum(jnp.where(lo, 0.0, sq), axis=-1, keepdims=True)
        r = jnp.where(lo, lax.rsqrt(s_lo * inv + EPS), lax.rsqrt(s_hi * inv + EPS))
        return a * r * gain

    q = _dot(h, w_ref[:, 0:d])
    for c in range(d // LANES):
        sl = slice(c * LANES, (c + 1) * LANES)
        qb_ref[:, sl] = (half_norm(q[:, sl], qg_ref[:, sl]) * q_scale).astype(BF16)
    k = _dot(h, w_ref[:, d:2 * d])
    for c in range(d // LANES):
        sl = slice(c * LANES, (c + 1) * LANES)
        kn = half_norm(k[:, sl], kg_ref[:, sl])
        kf_ref[:, sl] = kn
        kb_ref[:, sl] = kn.astype(BF16)
    v = _dot(h, w_ref[:, 2 * d:3 * d])
    vf_ref[...] = v
    vb_ref[...] = v.astype(BF16)


def _qkv(x, g, w, qg, kg, *, tm, half):
    m, d = x.shape
    row = lambda i: (i, 0)
    outs = [jax.ShapeDtypeStruct((m, d), t) for t in (BF16, F32, BF16, F32, BF16)]
    return pl.pallas_call(
        functools.partial(_qkv_kernel, d=d, half=half, q_scale=half ** -0.5 * LOG2E),
        grid=(m // tm,),
        in_specs=[pl.BlockSpec((tm, d), row), _resident((1, d)), _resident(w.shape),
                  _resident((1, d)), _resident((1, d))],
        out_specs=[pl.BlockSpec((tm, d), row)] * 5,
        out_shape=outs,
        compiler_params=_params(("parallel",)),
        name="qkv_proj",
    )(x, g, w, qg, kg)


def _lambda_11(l_ref, lam_init):
    a = jnp.sum(l_ref[0:1, :] * l_ref[1:2, :], axis=-1, keepdims=True)
    b = jnp.sum(l_ref[2:3, :] * l_ref[3:4, :], axis=-1, keepdims=True)
    return jnp.exp(a) - jnp.exp(b) + lam_init


def _subln_bf16(o, sg, lam_init):
    r = lax.rsqrt(jnp.mean(o * o, axis=-1, keepdims=True) + EPS)
    return (o * r * sg * (1.0 - lam_init)).astype(BF16)


def _softmax_step(s, v, m, l, acc):
    m_new = jnp.maximum(m, jnp.max(s, axis=-1, keepdims=True))
    alpha = jnp.exp2(m - m_new)
    p = jnp.exp2(s - m_new)
    l_new = alpha * l + jnp.sum(p, axis=-1, keepdims=True)
    acc_new = alpha * acc + _dot(p.astype(BF16), v)
    return m_new, l_new, acc_new


def _attn_prompt_kernel(slope_ref, lam_ref, q_ref, k_ref, v_ref, sg_ref, o_ref,
                        *, seq, tq, tk, half, lam_init):
    head = pl.program_id(1)
    slope = slope_ref[head]
    lam = _lambda_11(lam_ref, lam_init)
    lo = lax.broadcasted_iota(jnp.int32, (tq, LANES), 1) < half
    col = lax.broadcasted_iota(jnp.int32, (1, tk), 1)
    row2 = lax.broadcasted_iota(jnp.int32, (2 * tq, tk), 0) & (tq - 1)
    col2 = lax.broadcasted_iota(jnp.int32, (2 * tq, tk), 1)
    zero = jnp.zeros((), BF16)

    for qi in range(seq // tq):
        q0 = qi * tq
        q = q_ref[q0:q0 + tq, :]
        q2 = jnp.concatenate([jnp.where(lo, q, zero), jnp.where(lo, zero, q)], axis=0)

        def tile(k0, carry, masked):
            k = k_ref[pl.ds(k0, tk), :]
            v = v_ref[pl.ds(k0, tk), :]
            bias = slope * (col + (k0 - q0)).astype(F32)
            s = _dot_nt(q2, k) + bias
            if masked:
                s = jnp.where(col2 + (k0 - q0) <= row2, s, NEG)
            return _softmax_step(s, v, *carry)

        carry = (jnp.full((2 * tq, 1), NEG, F32), jnp.zeros((2 * tq, 1), F32),
                 jnp.zeros((2 * tq, LANES), F32))
        n_full = q0 // tk
        if n_full > 0:
            carry = lax.fori_loop(
                0, n_full, lambda i, c: tile(pl.multiple_of(i * tk, tk), c, False), carry)
        for kd in range(tq // tk):
            carry = tile(q0 + kd * tk, carry, True)
        _, l, acc = carry
        t = acc / l
        o = t[:tq] - lam * t[tq:]
        o_ref[q0:q0 + tq, :] = _subln_bf16(o, sg_ref[...], lam_init)


def _attn_prompt(slopes, lam4, qb, kb, vb, sg, *, batch, seq, heads, half, lam_init, tq, tk):
    m, d = qb.shape
    blk = pl.BlockSpec((seq, LANES), lambda b, h: (b, h))
    return pl.pallas_call(
        functools.partial(_attn_prompt_kernel, seq=seq, tq=tq, tk=tk, half=half,
                          lam_init=lam_init),
        grid=(batch, heads),
        in_specs=[pl.BlockSpec(memory_space=pltpu.SMEM), _resident(lam4.shape),
                  blk, blk, blk, _resident(sg.shape)],
        out_specs=blk,
        out_shape=jax.ShapeDtypeStruct((m, d), BF16),
        compiler_params=_params(("parallel", "parallel")),
        name="attn_prompt",
    )(slopes, lam4, qb, kb, vb, sg)


def _page_part(s):
    m = jnp.max(s, axis=-1, keepdims=True)
    p = jnp.exp2(s - m)
    return m, jnp.sum(p, axis=-1, keepdims=True), p


def _attn_sample_kernel(pt_ref, lam_ref, q2_ref, kn_ref, vn_ref, sg_ref, mb_ref, mbn_ref, sc_ref,
                        *rest, pages, past, lam_init):
    k_refs = rest[:pages]
    v_refs = rest[pages:2 * pages]
    o_ref, m_scr, l_scr, acc_scr = rest[2 * pages:]
    j = pl.program_id(1)
    nr = mb_ref.shape[0]

    @pl.when(j == 0)
    def _():
        m_scr[...] = jnp.full(m_scr.shape, NEG, F32)
        l_scr[...] = jnp.zeros(l_scr.shape, F32)
        acc_scr[...] = jnp.zeros(acc_scr.shape, F32)

    def merge(parts):
        m_old = m_scr[...]
        m_new = m_old
        for m, _, _ in parts:
            m_new = jnp.maximum(m_new, m)
        a = jnp.exp2(m_old - m_new)
        l = a * l_scr[...]
        acc = a * acc_scr[...]
        for m, lp, op in parts:
            w = jnp.exp2(m - m_new)
            l = l + w * lp
            acc = acc + w * op
        m_scr[...] = m_new
        l_scr[...] = l
        acc_scr[...] = acc

    q2 = q2_ref[...]
    mb = mb_ref[...]
    sc = sc_ref[...]
    parts = []
    for pr in range(pages // 2):
        ka, kb, va, vb = k_refs[2 * pr], k_refs[2 * pr + 1], v_refs[2 * pr], v_refs[2 * pr + 1]
        kcat = jnp.concatenate([ka[...].astype(BF16), kb[...].astype(BF16)], axis=1)
        s2 = _dot_nt(q2, kcat)
        ma, la, pa = _page_part(s2[:nr] + mb)
        mb_, lb, pb = _page_part(s2[nr:] + mb)
        vcat = jnp.concatenate([va[...].astype(BF16), vb[...].astype(BF16)], axis=1)
        o2 = _dot(jnp.concatenate([pa, pb], axis=0).astype(BF16), vcat)
        off = ((j * pages + 2 * pr) * PAGE_SIZE - past).astype(F32)
        parts.append((ma + sc * off, la, o2[:nr, :LANES]))
        parts.append((mb_ + sc * (off + PAGE_SIZE), lb, o2[nr:, LANES:]))
    merge(parts)

    @pl.when(j == pl.num_programs(1) - 1)
    def _():
        mn, ln, pn = _page_part(_dot_nt(q2[:nr, :LANES], kn_ref[...]) + mbn_ref[...])
        merge([(mn, ln, _dot(pn.astype(BF16), vn_ref[...]))])
        t = acc_scr[...] / l_scr[...]
        o = t[:nr // 2] - _lambda_11(lam_ref, lam_init) * t[nr // 2:]
        o_ref[...] = _subln_bf16(o, sg_ref[...], lam_init)


def _attn_sample(page_table, slopes, lam4, qbs, kbs, vbs, sg, ck, cv, *, layer, heads, dec_seq,
                 half, lam_init):
    nb = page_table.shape[0]
    d = qbs.shape[1]
    n_pages = page_table.shape[1]
    pages = PAGES_PER_STEP
    past = n_pages * PAGE_SIZE
    hq = heads * dec_seq
    nr = 2 * hq
    q4 = qbs.reshape(nb, dec_seq, heads, LANES).transpose(0, 2, 1, 3).reshape(nb, hq, LANES)
    lo = jnp.arange(LANES) < half
    zq = jnp.zeros_like(q4)
    qa = jnp.concatenate([jnp.where(lo, q4, zq), jnp.where(lo, zq, q4)], axis=1)
    za = jnp.zeros_like(qa)
    q2 = jnp.concatenate([jnp.concatenate([qa, za], axis=2),
                          jnp.concatenate([za, qa], axis=2)], axis=1)
    n_new = LANES
    pad = ((0, 0), (0, n_new - hq), (0, 0))
    kn = jnp.pad(kbs.reshape(nb, hq, LANES), pad)
    vn = jnp.pad(vbs.reshape(nb, hq, LANES), pad)
    r = jnp.arange(nr)
    r_head, r_query = (r % hq) // dec_seq, r % dec_seq
    r_slope = slopes[r_head][:, None]

    def table(n_cols, causal):
        c = jnp.arange(n_cols)
        c_head, c_tok = c % heads, c // heads
        ok = c_head[None, :] == r_head[:, None]
        if causal:
            ok = ok & (c_tok[None, :] <= r_query[:, None])
        return jnp.where(ok, r_slope * c_tok[None, :].astype(F32), NEG)

    mb = table(PAGE_SIZE * heads, False)
    mbn = table(n_new, True)

    per_b = lambda b, j, pt: (b, 0, 0)
    const2 = lambda b, j, pt: (0, 0)
    page_specs = [
        pl.BlockSpec((None, None, PAGE_SIZE * heads, LANES),
                     functools.partial(lambda b, j, pt, i: (layer, pt[b, j * pages + i], 0, 0), i=i))
        for i in range(pages)]
    grid_spec = pltpu.PrefetchScalarGridSpec(
        num_scalar_prefetch=1,
        grid=(nb, n_pages // pages),
        in_specs=[pl.BlockSpec(lam4.shape, const2),
                  pl.BlockSpec((None, 2 * nr, 2 * LANES), per_b),
                  pl.BlockSpec((None, n_new, LANES), per_b),
                  pl.BlockSpec((None, n_new, LANES), per_b),
                  pl.BlockSpec(sg.shape, const2),
                  pl.BlockSpec(mb.shape, const2),
                  pl.BlockSpec(mbn.shape, const2),
                  pl.BlockSpec(r_slope.shape, const2)] + page_specs + page_specs,
        out_specs=pl.BlockSpec((None, hq, LANES), per_b),
        scratch_shapes=[pltpu.VMEM((nr, 1), F32), pltpu.VMEM((nr, 1), F32),
                        pltpu.VMEM((nr, LANES), F32)],
    )
    out = pl.pallas_call(
        functools.partial(_attn_sample_kernel, pages=pages, past=past, lam_init=lam_init),
        grid_spec=grid_spec,
        out_shape=jax.ShapeDtypeStruct((nb, hq, LANES), BF16),
        compiler_params=_params(("parallel", "arbitrary")),
        name="attn_sample",
    )(page_table, lam4, q2, kn, vn, sg, mb, mbn, r_slope, *([ck] * pages), *([cv] * pages))
    return out.reshape(nb, heads, dec_seq, LANES).transpose(0, 2, 1, 3).reshape(nb * dec_seq, d)


def _proj_mlp_kernel(x_ref, a_ref, wo_ref, g_ref, wu_ref, wd_ref, o_ref, *, ff_chunk):
    x1 = x_ref[...] + _dot(a_ref[...], wo_ref[...])
    h = _rmsnorm_bf16(x1, g_ref[...])
    acc = x1
    for c in range(wu_ref.shape[1] // ff_chunk):
        sl = slice(c * ff_chunk, (c + 1) * ff_chunk)
        u = jnp.maximum(_dot(h, wu_ref[:, sl]), 0.0)
        acc = acc + _dot((u * u).astype(BF16), wd_ref[sl, :])
    o_ref[...] = acc


def _proj_mlp(x, a, wo, g, wu, wd, *, tm):
    m, d = x.shape
    row = lambda i: (i, 0)
    return pl.pallas_call(
        functools.partial(_proj_mlp_kernel, ff_chunk=min(1024, wu.shape[1])),
        grid=(m // tm,),
        in_specs=[pl.BlockSpec((tm, d), row), pl.BlockSpec((tm, a.shape[1]), row),
                  _resident(wo.shape), _resident((1, d)), _resident(wu.shape), _resident(wd.shape)],
        out_specs=pl.BlockSpec((tm, d), row),
        out_shape=jax.ShapeDtypeStruct((m, d), F32),
        compiler_params=_params(("parallel",)),
        name="proj_mlp",
    )(x, a, wo, g, wu, wd)


def _ret_in_kernel(x_ref, g_ref, w_ref, z_ref, *, n_chunk):
    h = _rmsnorm_bf16(x_ref[...], g_ref[...])
    for c in range(w_ref.shape[1] // n_chunk):
        sl = slice(c * n_chunk, (c + 1) * n_chunk)
        z_ref[:, sl] = _dot(h, w_ref[:, sl]).astype(BF16)


def _ret_in(x, g, w, *, tm):
    m, d = x.shape
    n = w.shape[1]
    row = lambda i: (i, 0)
    return pl.pallas_call(
        functools.partial(_ret_in_kernel, n_chunk=min(1024, n)),
        grid=(m // tm,),
        in_specs=[pl.BlockSpec((tm, d), row), _resident((1, d)), _resident(w.shape)],
        out_specs=pl.BlockSpec((tm, n), row),
        out_shape=jax.ShapeDtypeStruct((m, n), BF16),
        compiler_params=_params(("parallel",)),
        name="ret_in_proj",
    )(x, g, w)


def _ret_tables(lg, chunk_len, k_scale):
    c = RET_CHUNK
    idx = lax.broadcasted_iota(jnp.int32, (c, 1), 0).astype(F32)
    diff = (lax.broadcasted_iota(jnp.int32, (c, c), 0)
            - lax.broadcasted_iota(jnp.int32, (c, c), 1)).astype(F32)
    dmat = jnp.where(diff >= 0, jnp.exp(lg * jnp.maximum(diff, 0.0)), 0.0) * k_scale
    q_dec = jnp.exp(lg * (idx + 1.0))
    k_dec = jnp.exp(lg * (chunk_len - 1.0 - idx)) * k_scale
    s_dec = jnp.exp(jnp.full((1, 1), lg * chunk_len, F32))
    return dmat, q_dec, k_dec, s_dec


def _ret_chunk(q, k, v, state, tables):
    dmat, q_dec, k_dec, s_dec = tables
    qk = _dot_nt(q, k) * dmat
    o = _dot(qk.astype(BF16), v) + q_dec * _dot(q, state.astype(BF16))
    kd_t = (k.astype(F32) * k_dec).T.astype(BF16)
    return o, s_dec * state + _dot(kd_t, v)


def _ret_gate(o, g):
    y = o * lax.rsqrt(jnp.mean(o * o, axis=-1, keepdims=True) + EPS)
    g = g.astype(F32)
    return (g * jax.nn.sigmoid(g) * y).astype(BF16)


def _retention_prompt_kernel(lg_ref, q_ref, k_ref, v_ref, g_ref, y_ref, s_ref, *, n_chunks,
                             k_scale):
    c = RET_CHUNK
    tables = _ret_tables(lg_ref[pl.program_id(1)], c, k_scale)
    s_ref[...] = jnp.zeros(s_ref.shape, F32)

    def chunk(i, _):
        rows = pl.ds(pl.multiple_of(i * c, c), c)
        o, s_ref[...] = _ret_chunk(q_ref[rows, :], k_ref[rows, :], v_ref[rows, :], s_ref[...],
                                   tables)
        y_ref[rows, :] = _ret_gate(o, g_ref[rows, :])
        return 0

    lax.fori_loop(0, n_chunks, chunk, 0, unroll=2)


def _retention_prompt(lg, z, *, n_seq, seq, heads, dk, dv):
    v_off = 2 * heads * dk // dv
    in_specs = [pl.BlockSpec(memory_space=pltpu.SMEM),
                pl.BlockSpec((seq, dk), lambda b, h: (b, h)),
                pl.BlockSpec((seq, dk), lambda b, h: (b, heads + h)),
                pl.BlockSpec((seq, dv), lambda b, h: (b, v_off + h)),
                pl.BlockSpec((seq, dv), lambda b, h: (b, v_off + heads + h))]
    return pl.pallas_call(
        functools.partial(_retention_prompt_kernel, n_chunks=seq // RET_CHUNK, k_scale=dk ** -0.5),
        grid=(n_seq, heads),
        in_specs=in_specs,
        out_specs=[pl.BlockSpec((seq, dv), lambda b, h: (b, h)),
                   pl.BlockSpec((None, None, dk, dv), lambda b, h: (b, h, 0, 0))],
        out_shape=[jax.ShapeDtypeStruct((z.shape[0], heads * dv), BF16),
                   jax.ShapeDtypeStruct((n_seq, heads, dk, dv), F32)],
        compiler_params=_params(("parallel", "parallel")),
        name="retention_prompt",
    )(lg, z, z, z, z)


def _retention_sample_kernel(lg_ref, z_ref, s0_ref, y_ref, s_ref, *, heads, dk, dv, chunk_len,
                             k_scale):
    rows = z_ref.shape[0]

    def padded(lo, width):
        a = z_ref[:, lo:lo + width]
        return jnp.concatenate([a, jnp.zeros((RET_CHUNK - rows, width), a.dtype)], axis=0)

    for h in range(heads):
        tables = _ret_tables(lg_ref[h], chunk_len, k_scale)
        v_lo = 2 * heads * dk + h * dv
        o, s_ref[h] = _ret_chunk(padded(h * dk, dk), padded((heads + h) * dk, dk),
                                 padded(v_lo, dv), s0_ref[h], tables)
        y_ref[:, h * dv:(h + 1) * dv] = _ret_gate(
            o[:rows], z_ref[:, v_lo + heads * dv:v_lo + (heads + 1) * dv])


def _retention_sample(lg, z, state, *, chunk_len):
    n_seq, rows, n = z.shape
    _, heads, dk, dv = state.shape
    per_b3 = lambda b: (b, 0, 0)
    per_b4 = lambda b: (b, 0, 0, 0)
    return pl.pallas_call(
        functools.partial(_retention_sample_kernel, heads=heads, dk=dk, dv=dv, chunk_len=chunk_len,
                          k_scale=dk ** -0.5),
        grid=(n_seq,),
        in_specs=[pl.BlockSpec(memory_space=pltpu.SMEM),
                  pl.BlockSpec((None, rows, n), per_b3),
                  pl.BlockSpec((None, heads, dk, dv), per_b4)],
        out_specs=[pl.BlockSpec((None, rows, heads * dv), per_b3),
                   pl.BlockSpec((None, heads, dk, dv), per_b4)],
        out_shape=[jax.ShapeDtypeStruct((n_seq, rows, heads * dv), BF16),
                   jax.ShapeDtypeStruct(state.shape, F32)],
        compiler_params=_params(("parallel",)),
        name="retention_sample",
    )(lg, z, state)


def _row_tile(m):
    return min(m, 512)


def kernel(x_prompt, x_sample, cache_k, cache_v, state_ret, page_table, norm_mix, norm_ffn,
           w_attn_in, q_norm_g, k_norm_g, lambda_q1, lambda_k1, lambda_q2, lambda_k2, subln_g,
           w_attn_out, w_ret_in, w_ret_out, w_up, w_down):
    batch, seq, d = x_prompt.shape
    dec_batch, dec_seq, _ = x_sample.shape
    depth = norm_mix.shape[0]
    half = q_norm_g.shape[1]
    heads_a = d // (2 * half)
    heads_r, dk, dv = state_ret.shape[2:]
    n_pool = cache_k.shape[1]
    assert 2 * half == LANES and cache_k.shape[2] == PAGE_SIZE and dec_seq & (dec_seq - 1) == 0

    xp = x_prompt.reshape(batch * seq, d)
    xs = x_sample.reshape(dec_batch * dec_seq, d)
    tp, ts = _row_tile(xp.shape[0]), _row_tile(xs.shape[0])
    ck = cache_k.reshape(cache_k.shape[0], n_pool, PAGE_SIZE * heads_a, LANES)
    cv = cache_v.reshape(cache_v.shape[0], n_pool, PAGE_SIZE * heads_a, LANES)
    slopes = 2.0 ** (-8.0 * jnp.arange(1, heads_a + 1, dtype=F32) / heads_a) * LOG2E
    log_gamma = jnp.log1p(-(2.0 ** (-5.0 - jnp.arange(heads_r, dtype=F32))))

    kp_l, vp_l, ks_l, vs_l, sp_l, ss_l = [], [], [], [], [], []
    for i in range(depth):
        g_mix = norm_mix[i].reshape(1, d)
        g_ffn = norm_ffn[i].reshape(1, d)
        wu, wd = w_up[i].astype(BF16), w_down[i].astype(BF16)
        if i % 2 == 0:
            a = i // 2
            lam_init = 0.8 - 0.6 * math.exp(-0.3 * i)
            w_in, wo = w_attn_in[a].astype(BF16), w_attn_out[a].astype(BF16)
            qg = jnp.tile(q_norm_g[a], d // half).reshape(1, d)
            kg = jnp.tile(k_norm_g[a], d // half).reshape(1, d)
            sg = subln_g[a].reshape(1, LANES)
            lam4 = jnp.stack([lambda_q1[a], lambda_k1[a], lambda_q2[a], lambda_k2[a]])
            qb, kf, kb, vf, vb = _qkv(xp, g_mix, w_in, qg, kg, tm=tp, half=half)
            qbs, kfs, kbs, vfs, vbs = _qkv(xs, g_mix, w_in, qg, kg, tm=ts, half=half)
            op = _attn_prompt(slopes, lam4, qb, kb, vb, sg, batch=batch, seq=seq, heads=heads_a,
                              half=half, lam_init=lam_init, tq=min(512, seq), tk=min(512, seq))
            os_ = _attn_sample(page_table, slopes, lam4, qbs, kbs, vbs, sg, ck, cv, layer=a,
                               heads=heads_a, dec_seq=dec_seq, half=half, lam_init=lam_init)
            kp_l.append(kf.reshape(batch, seq, heads_a, LANES))
            vp_l.append(vf.reshape(batch, seq, heads_a, LANES))
            ks_l.append(kfs.reshape(dec_batch, dec_seq, heads_a, LANES))
            vs_l.append(vfs.reshape(dec_batch, dec_seq, heads_a, LANES))
        else:
            r = i // 2
            w_in, wo = w_ret_in[r].astype(BF16), w_ret_out[r].astype(BF16)
            zp = _ret_in(xp, g_mix, w_in, tm=tp)
            zs = _ret_in(xs, g_mix, w_in, tm=ts)
            op, sp = _retention_prompt(log_gamma, zp, n_seq=batch, seq=seq, heads=heads_r,
                                       dk=dk, dv=dv)
            zs = jnp.pad(zs.reshape(dec_batch, dec_seq, -1),
                         ((0, 0), (0, SAMPLE_ROWS - dec_seq), (0, 0)))
            os_, ss = _retention_sample(log_gamma, zs, state_ret[r], chunk_len=dec_seq)
            os_ = os_[:, :dec_seq].reshape(dec_batch * dec_seq, -1)
            sp_l.append(sp)
            ss_l.append(ss)
        xp = _proj_mlp(xp, op, wo, g_ffn, wu, wd, tm=tp)
        xs = _proj_mlp(xs, os_, wo, g_ffn, wu, wd, tm=ts)

    return (xp.reshape(batch, seq, d), xs.reshape(dec_batch, dec_seq, d),
            jnp.stack(kp_l), jnp.stack(vp_l), jnp.stack(ks_l), jnp.stack(vs_l),
            jnp.stack(sp_l), jnp.stack(ss_l))
```

```python
import functools
import math

import jax
import jax.numpy as jnp
from jax import lax
from jax.experimental import pallas as pl
from jax.experimental.pallas import tpu as pltpu

F32 = jnp.float32
BF16 = jnp.bfloat16
EPS = 1e-6
NEG = -1e30
LANES = 128
VMEM_LIMIT = 56 * 1024 * 1024
PAGES_PER_STEP = 8
PAGE_SLOTS = 3
RET_CHUNK = 128
PAGE_SIZE = 128
SAMPLE_ROWS = 16
LOG2E = math.log2(math.e)


def _params(sem):
    return pltpu.CompilerParams(dimension_semantics=sem, vmem_limit_bytes=VMEM_LIMIT)


def _resident(shape):
    nd = len(shape)
    return pl.BlockSpec(shape, lambda *_: (0,) * nd, pipeline_mode=pl.Buffered(1))


def _rmsnorm_bf16(x, g):
    return (x * lax.rsqrt(jnp.mean(x * x, axis=-1, keepdims=True) + EPS) * g).astype(BF16)


def _dot(a, b):
    return jnp.dot(a, b, preferred_element_type=F32)


def _dot_nt(a, b):
    return lax.dot_general(a, b, (((1,), (1,)), ((), ())), preferred_element_type=F32)


def _qkv_kernel(x_ref, g_ref, w_ref, qg_ref, kg_ref, qb_ref, kf_ref, kb_ref, vf_ref, vb_ref,
                *, d, half, q_scale):
    h = _rmsnorm_bf16(x_ref[...], g_ref[...])
    tm = h.shape[0]
    lo = lax.broadcasted_iota(jnp.int32, (tm, LANES), 1) < half
    inv = 1.0 / half

    def half_norm(a, gain):
        sq = a * a
        s_lo = jnp.sum(jnp.where(lo, sq, 0.0), axis=-1, keepdims=True)
        s_hi = jnp.sum(jnp.where(lo, 0.0, sq), axis=-1, keepdims=True)
        r = jnp.where(lo, lax.rsqrt(s_lo * inv + EPS), lax.rsqrt(s_hi * inv + EPS))
        return a * r * gain

    q = _dot(h, w_ref[:, 0:d])
    for c in range(d // LANES):
        sl = slice(c * LANES, (c + 1) * LANES)
        qb_ref[:, sl] = (half_norm(q[:, sl], qg_ref[:, sl]) * q_scale).astype(BF16)
    k = _dot(h, w_ref[:, d:2 * d])
    for c in range(d // LANES):
        sl = slice(c * LANES, (c + 1) * LANES)
        kn = half_norm(k[:, sl], kg_ref[:, sl])
        kf_ref[:, sl] = kn
        kb_ref[:, sl] = kn.astype(BF16)
    v = _dot(h, w_ref[:, 2 * d:3 * d])
    vf_ref[...] = v
    vb_ref[...] = v.astype(BF16)


def _qkv(x, g, w, qg, kg, *, tm, half):
    m, d = x.shape
    row = lambda i: (i, 0)
    outs = [jax.ShapeDtypeStruct((m, d), t) for t in (BF16, F32, BF16, F32, BF16)]
    return pl.pallas_call(
        functools.partial(_qkv_kernel, d=d, half=half, q_scale=half ** -0.5 * LOG2E),
        grid=(m // tm,),
        in_specs=[pl.BlockSpec((tm, d), row), _resident((1, d)), _resident(w.shape),
                  _resident((1, d)), _resident((1, d))],
        out_specs=[pl.BlockSpec((tm, d), row)] * 5,
        out_shape=outs,
        compiler_params=_params(("parallel",)),
        name="qkv_proj",
    )(x, g, w, qg, kg)


def _lambda_11(l_ref, lam_init):
    a = jnp.sum(l_ref[0:1, :] * l_ref[1:2, :], axis=-1, keepdims=True)
    b = jnp.sum(l_ref[2:3, :] * l_ref[3:4, :], axis=-1, keepdims=True)
    return jnp.exp(a) - jnp.exp(b) + lam_init


def _subln_bf16(o, sg, lam_init):
    r = lax.rsqrt(jnp.mean(o * o, axis=-1, keepdims=True) + EPS)
    return (o * r * sg * (1.0 - lam_init)).astype(BF16)


def _softmax_step(s, v, m, l, acc):
    m_new = jnp.maximum(m, jnp.max(s, axis=-1, keepdims=True))
    alpha = jnp.exp2(m - m_new)
    p = jnp.exp2(s - m_new)
    l_new = alpha * l + jnp.sum(p, axis=-1, keepdims=True)
    acc_new = alpha * acc + _dot(p.astype(BF16), v)
    return m_new, l_new, acc_new


def _attn_prompt_kernel(slope_ref, lam_ref, q_ref, k_ref, v_ref, sg_ref, o_ref,
                        *, seq, tq, tk, half, lam_init):
    head = pl.program_id(1)
    slope = slope_ref[head]
    lam = _lambda_11(lam_ref, lam_init)
    lo = lax.broadcasted_iota(jnp.int32, (tq, LANES), 1) < half
    col = lax.broadcasted_iota(jnp.int32, (1, tk), 1)
    row2 = lax.broadcasted_iota(jnp.int32, (2 * tq, tk), 0) & (tq - 1)
    col2 = lax.broadcasted_iota(jnp.int32, (2 * tq, tk), 1)
    zero = jnp.zeros((), BF16)

    for qi in range(seq // tq):
        q0 = qi * tq
        q = q_ref[q0:q0 + tq, :]
        q2 = jnp.concatenate([jnp.where(lo, q, zero), jnp.where(lo, zero, q)], axis=0)

        def tile(k0, carry, masked):
            k = k_ref[pl.ds(k0, tk), :]
            v = v_ref[pl.ds(k0, tk), :]
            bias = slope * (col + (k0 - q0)).astype(F32)
            s = _dot_nt(q2, k) + bias
            if masked:
                s = jnp.where(col2 + (k0 - q0) <= row2, s, NEG)
            return _softmax_step(s, v, *carry)

        carry = (jnp.full((2 * tq, 1), NEG, F32), jnp.zeros((2 * tq, 1), F32),
                 jnp.zeros((2 * tq, LANES), F32))
        n_full = q0 // tk
        if n_full > 0:
            carry = lax.fori_loop(
                0, n_full, lambda i, c: tile(pl.multiple_of(i * tk, tk), c, False), carry)
        for kd in range(tq // tk):
            carry = tile(q0 + kd * tk, carry, True)
        _, l, acc = carry
        t = acc / l
        o = t[:tq] - lam * t[tq:]
        o_ref[q0:q0 + tq, :] = _subln_bf16(o, sg_ref[...], lam_init)


def _attn_prompt(slopes, lam4, qb, kb, vb, sg, *, batch, seq, heads, half, lam_init, tq, tk):
    m, d = qb.shape
    blk = pl.BlockSpec((seq, LANES), lambda b, h: (b, h))
    return pl.pallas_call(
        functools.partial(_attn_prompt_kernel, seq=seq, tq=tq, tk=tk, half=half,
                          lam_init=lam_init),
        grid=(batch, heads),
        in_specs=[pl.BlockSpec(memory_space=pltpu.SMEM), _resident(lam4.shape),
                  blk, blk, blk, _resident(sg.shape)],
        out_specs=blk,
        out_shape=jax.ShapeDtypeStruct((m, d), BF16),
        compiler_params=_params(("parallel", "parallel")),
        name="attn_prompt",
    )(slopes, lam4, qb, kb, vb, sg)


def _page_part(s):
    m = jnp.max(s, axis=-1, keepdims=True)
    p = jnp.exp2(s - m)
    return m, jnp.sum(p, axis=-1, keepdims=True), p


def _attn_sample_kernel(pt_ref, lam_ref, q2_ref, kn_ref, vn_ref, sg_ref, mb_ref, mbn_ref, sc_ref,
                        ck_ref, cv_ref, o_ref, k_buf, v_buf, sem, m_scr, l_scr, acc_scr,
                        *, layer, pages, past, lam_init):
    j = pl.program_id(1)
    n_j = pl.num_programs(1)
    step = pl.program_id(0) * n_j + j
    n_steps = pl.num_programs(0) * n_j
    nr = mb_ref.shape[0]

    def page_copies(s):
        slot = s % PAGE_SLOTS
        sb, sj = s // n_j, s % n_j
        out = []
        for i in range(pages):
            page = pt_ref[sb, sj * pages + i]
            out.append(pltpu.make_async_copy(ck_ref.at[layer, page], k_buf.at[slot, i],
                                             sem.at[slot, i]))
            out.append(pltpu.make_async_copy(cv_ref.at[layer, page], v_buf.at[slot, i],
                                             sem.at[slot, pages + i]))
        return out

    @pl.when(step == 0)
    def _():
        for s in range(PAGE_SLOTS - 1):
            for c in page_copies(s):
                c.start()

    @pl.when(step + PAGE_SLOTS - 1 < n_steps)
    def _():
        for c in page_copies(step + PAGE_SLOTS - 1):
            c.start()

    @pl.when(j == 0)
    def _():
        m_scr[...] = jnp.full(m_scr.shape, NEG, F32)
        l_scr[...] = jnp.zeros(l_scr.shape, F32)
        acc_scr[...] = jnp.zeros(acc_scr.shape, F32)

    def merge(parts):
        m_old = m_scr[...]
        m_new = m_old
        for m, _, _ in parts:
            m_new = jnp.maximum(m_new, m)
        a = jnp.exp2(m_old - m_new)
        l = a * l_scr[...]
        acc = a * acc_scr[...]
        for m, lp, op in parts:
            w = jnp.exp2(m - m_new)
            l = l + w * lp
            acc = acc + w * op
        m_scr[...] = m_new
        l_scr[...] = l
        acc_scr[...] = acc

    for c in page_copies(step):
        c.wait()
    slot = step % PAGE_SLOTS
    q2 = q2_ref[...]
    mb = mb_ref[...]
    sc = sc_ref[...]
    parts = []
    for pr in range(pages // 2):
        ia, ib = 2 * pr, 2 * pr + 1
        kcat = jnp.concatenate([k_buf[slot, ia].astype(BF16), k_buf[slot, ib].astype(BF16)], axis=1)
        s2 = _dot_nt(q2, kcat)
        ma, la, pa = _page_part(s2[:nr] + mb)
        mb_, lb, pb = _page_part(s2[nr:] + mb)
        vcat = jnp.concatenate([v_buf[slot, ia].astype(BF16), v_buf[slot, ib].astype(BF16)], axis=1)
        o2 = _dot(jnp.concatenate([pa, pb], axis=0).astype(BF16), vcat)
        off = ((j * pages + ia) * PAGE_SIZE - past).astype(F32)
        parts.append((ma + sc * off, la, o2[:nr, :LANES]))
        parts.append((mb_ + sc * (off + PAGE_SIZE), lb, o2[nr:, LANES:]))
    merge(parts)

    @pl.when(j == n_j - 1)
    def _():
        mn, ln, pn = _page_part(_dot_nt(q2[:nr, :LANES], kn_ref[...]) + mbn_ref[...])
        merge([(mn, ln, _dot(pn.astype(BF16), vn_ref[...]))])
        t = acc_scr[...] / l_scr[...]
        o = t[:nr // 2] - _lambda_11(lam_ref, lam_init) * t[nr // 2:]
        o_ref[...] = _subln_bf16(o, sg_ref[...], lam_init)


def _attn_sample(page_table, slopes, lam4, qbs, kbs, vbs, sg, ck, cv, *, layer, heads, dec_seq,
                 half, lam_init):
    nb = page_table.shape[0]
    d = qbs.shape[1]
    n_pages = page_table.shape[1]
    pages = PAGES_PER_STEP
    past = n_pages * PAGE_SIZE
    hq = heads * dec_seq
    nr = 2 * hq
    assert nb * (n_pages // pages) >= PAGE_SLOTS - 1 and n_pages % pages == 0
    q4 = qbs.reshape(nb, dec_seq, heads, LANES).transpose(0, 2, 1, 3).reshape(nb, hq, LANES)
    lo = jnp.arange(LANES) < half
    zq = jnp.zeros_like(q4)
    qa = jnp.concatenate([jnp.where(lo, q4, zq), jnp.where(lo, zq, q4)], axis=1)
    za = jnp.zeros_like(qa)
    q2 = jnp.concatenate([jnp.concatenate([qa, za], axis=2),
                          jnp.concatenate([za, qa], axis=2)], axis=1)
    n_new = LANES
    pad = ((0, 0), (0, n_new - hq), (0, 0))
    kn = jnp.pad(kbs.reshape(nb, hq, LANES), pad)
    vn = jnp.pad(vbs.reshape(nb, hq, LANES), pad)
    r = jnp.arange(nr)
    r_head, r_query = (r % hq) // dec_seq, r % dec_seq
    r_slope = slopes[r_head][:, None]

    def table(n_cols, causal):
        c = jnp.arange(n_cols)
        c_head, c_tok = c % heads, c // heads
        ok = c_head[None, :] == r_head[:, None]
        if causal:
            ok = ok & (c_tok[None, :] <= r_query[:, None])
        return jnp.where(ok, r_slope * c_tok[None, :].astype(F32), NEG)

    mb = table(PAGE_SIZE * heads, False)
    mbn = table(n_new, True)

    per_b = lambda b, j, pt: (b, 0, 0)
    const2 = lambda b, j, pt: (0, 0)
    page_rows = PAGE_SIZE * heads
    grid_spec = pltpu.PrefetchScalarGridSpec(
        num_scalar_prefetch=1,
        grid=(nb, n_pages // pages),
        in_specs=[pl.BlockSpec(lam4.shape, const2),
                  pl.BlockSpec((None, 2 * nr, 2 * LANES), per_b),
                  pl.BlockSpec((None, n_new, LANES), per_b),
                  pl.BlockSpec((None, n_new, LANES), per_b),
                  pl.BlockSpec(sg.shape, const2),
                  pl.BlockSpec(mb.shape, const2),
                  pl.BlockSpec(mbn.shape, const2),
                  pl.BlockSpec(r_slope.shape, const2),
                  pl.BlockSpec(memory_space=pl.ANY),
                  pl.BlockSpec(memory_space=pl.ANY)],
        out_specs=pl.BlockSpec((None, hq, LANES), per_b),
        scratch_shapes=[pltpu.VMEM((PAGE_SLOTS, pages, page_rows, LANES), F32),
                        pltpu.VMEM((PAGE_SLOTS, pages, page_rows, LANES), F32),
                        pltpu.SemaphoreType.DMA((PAGE_SLOTS, 2 * pages)),
                        pltpu.VMEM((nr, 1), F32), pltpu.VMEM((nr, 1), F32),
                        pltpu.VMEM((nr, LANES), F32)],
    )
    out = pl.pallas_call(
        functools.partial(_attn_sample_kernel, layer=layer, pages=pages, past=past,
                          lam_init=lam_init),
        grid_spec=grid_spec,
        out_shape=jax.ShapeDtypeStruct((nb, hq, LANES), BF16),
        compiler_params=_params(("arbitrary", "arbitrary")),
        name="attn_sample",
    )(page_table, lam4, q2, kn, vn, sg, mb, mbn, r_slope, ck, cv)
    return out.reshape(nb, heads, dec_seq, LANES).transpose(0, 2, 1, 3).reshape(nb * dec_seq, d)


def _proj_mlp_kernel(x_ref, a_ref, wo_ref, g_ref, wu_ref, wd_ref, o_ref, *, ff_chunk):
    x1 = x_ref[...] + _dot(a_ref[...], wo_ref[...])
    h = _rmsnorm_bf16(x1, g_ref[...])
    acc = x1
    for c in range(wu_ref.shape[1] // ff_chunk):
        sl = slice(c * ff_chunk, (c + 1) * ff_chunk)
        u = jnp.maximum(_dot(h, wu_ref[:, sl]), 0.0)
        acc = acc + _dot((u * u).astype(BF16), wd_ref[sl, :])
    o_ref[...] = acc


def _proj_mlp(x, a, wo, g, wu, wd, *, tm):
    m, d = x.shape
    row = lambda i: (i, 0)
    return pl.pallas_call(
        functools.partial(_proj_mlp_kernel, ff_chunk=min(1024, wu.shape[1])),
        grid=(m // tm,),
        in_specs=[pl.BlockSpec((tm, d), row), pl.BlockSpec((tm, a.shape[1]), row),
                  _resident(wo.shape), _resident((1, d)), _resident(wu.shape), _resident(wd.shape)],
        out_specs=pl.BlockSpec((tm, d), row),
        out_shape=jax.ShapeDtypeStruct((m, d), F32),
        compiler_params=_params(("parallel",)),
        name="proj_mlp",
    )(x, a, wo, g, wu, wd)


def _ret_in_kernel(x_ref, g_ref, w_ref, z_ref, *, n_chunk):
    h = _rmsnorm_bf16(x_ref[...], g_ref[...])
    for c in range(w_ref.shape[1] // n_chunk):
        sl = slice(c * n_chunk, (c + 1) * n_chunk)
        z_ref[:, sl] = _dot(h, w_ref[:, sl]).astype(BF16)


def _ret_in(x, g, w, *, tm):
    m, d = x.shape
    n = w.shape[1]
    row = lambda i: (i, 0)
    return pl.pallas_call(
        functools.partial(_ret_in_kernel, n_chunk=min(1024, n)),
        grid=(m // tm,),
        in_specs=[pl.BlockSpec((tm, d), row), _resident((1, d)), _resident(w.shape)],
        out_specs=pl.BlockSpec((tm, n), row),
        out_shape=jax.ShapeDtypeStruct((m, n), BF16),
        compiler_params=_params(("parallel",)),
        name="ret_in_proj",
    )(x, g, w)


def _ret_tables(lg, chunk_len, k_scale):
    c = RET_CHUNK
    idx = lax.broadcasted_iota(jnp.int32, (c, 1), 0).astype(F32)
    diff = (lax.broadcasted_iota(jnp.int32, (c, c), 0)
            - lax.broadcasted_iota(jnp.int32, (c, c), 1)).astype(F32)
    dmat = jnp.where(diff >= 0, jnp.exp(lg * jnp.maximum(diff, 0.0)), 0.0) * k_scale
    q_dec = jnp.exp(lg * (idx + 1.0))
    k_dec = jnp.exp(lg * (chunk_len - 1.0 - idx)) * k_scale
    s_dec = jnp.exp(jnp.full((1, 1), lg * chunk_len, F32))
    return dmat, q_dec, k_dec, s_dec


def _ret_chunk(q, k, v, state, tables):
    dmat, q_dec, k_dec, s_dec = tables
    qk = _dot_nt(q, k) * dmat
    o = _dot(qk.astype(BF16), v) + q_dec * _dot(q, state.astype(BF16))
    kd_t = (k.astype(F32) * k_dec).T.astype(BF16)
    return o, s_dec * state + _dot(kd_t, v)


def _ret_gate(o, g):
    y = o * lax.rsqrt(jnp.mean(o * o, axis=-1, keepdims=True) + EPS)
    g = g.astype(F32)
    return (g * jax.nn.sigmoid(g) * y).astype(BF16)


def _retention_prompt_kernel(lg_ref, q_ref, k_ref, v_ref, g_ref, y_ref, s_ref, *, n_chunks,
                             k_scale):
    c = RET_CHUNK
    tables = _ret_tables(lg_ref[pl.program_id(1)], c, k_scale)
    s_ref[...] = jnp.zeros(s_ref.shape, F32)

    def chunk(i, _):
        rows = pl.ds(pl.multiple_of(i * c, c), c)
        o, s_ref[...] = _ret_chunk(q_ref[rows, :], k_ref[rows, :], v_ref[rows, :], s_ref[...],
                                   tables)
        y_ref[rows, :] = _ret_gate(o, g_ref[rows, :])
        return 0

    lax.fori_loop(0, n_chunks, chunk, 0, unroll=2)


def _retention_prompt(lg, z, *, n_seq, seq, heads, dk, dv):
    v_off = 2 * heads * dk // dv
    in_specs = [pl.BlockSpec(memory_space=pltpu.SMEM),
                pl.BlockSpec((seq, dk), lambda b, h: (b, h)),
                pl.BlockSpec((seq, dk), lambda b, h: (b, heads + h)),
                pl.BlockSpec((seq, dv), lambda b, h: (b, v_off + h)),
                pl.BlockSpec((seq, dv), lambda b, h: (b, v_off + heads + h))]
    return pl.pallas_call(
        functools.partial(_retention_prompt_kernel, n_chunks=seq // RET_CHUNK, k_scale=dk ** -0.5),
        grid=(n_seq, heads),
        in_specs=in_specs,
        out_specs=[pl.BlockSpec((seq, dv), lambda b, h: (b, h)),
                   pl.BlockSpec((None, None, dk, dv), lambda b, h: (b, h, 0, 0))],
        out_shape=[jax.ShapeDtypeStruct((z.shape[0], heads * dv), BF16),
                   jax.ShapeDtypeStruct((n_seq, heads, dk, dv), F32)],
        compiler_params=_params(("parallel", "parallel")),
        name="retention_prompt",
    )(lg, z, z, z, z)


def _retention_sample_kernel(lg_ref, z_ref, s0_ref, y_ref, s_ref, *, heads, dk, dv, chunk_len,
                             k_scale):
    rows = z_ref.shape[0]

    def padded(lo, width):
        a = z_ref[:, lo:lo + width]
        return jnp.concatenate([a, jnp.zeros((RET_CHUNK - rows, width), a.dtype)], axis=0)

    for h in range(heads):
        tables = _ret_tables(lg_ref[h], chunk_len, k_scale)
        v_lo = 2 * heads * dk + h * dv
        o, s_ref[h] = _ret_chunk(padded(h * dk, dk), padded((heads + h) * dk, dk),
                                 padded(v_lo, dv), s0_ref[h], tables)
        y_ref[:, h * dv:(h + 1) * dv] = _ret_gate(
            o[:rows], z_ref[:, v_lo + heads * dv:v_lo + (heads + 1) * dv])


def _retention_sample(lg, z, state, *, chunk_len):
    n_seq, rows, n = z.shape
    _, heads, dk, dv = state.shape
    per_b3 = lambda b: (b, 0, 0)
    per_b4 = lambda b: (b, 0, 0, 0)
    return pl.pallas_call(
        functools.partial(_retention_sample_kernel, heads=heads, dk=dk, dv=dv, chunk_len=chunk_len,
                          k_scale=dk ** -0.5),
        grid=(n_seq,),
        in_specs=[pl.BlockSpec(memory_space=pltpu.SMEM),
                  pl.BlockSpec((None, rows, n), per_b3),
                  pl.BlockSpec((None, heads, dk, dv), per_b4)],
        out_specs=[pl.BlockSpec((None, rows, heads * dv), per_b3),
                   pl.BlockSpec((None, heads, dk, dv), per_b4)],
        out_shape=[jax.ShapeDtypeStruct((n_seq, rows, heads * dv), BF16),
                   jax.ShapeDtypeStruct(state.shape, F32)],
        compiler_params=_params(("parallel",)),
        name="retention_sample",
    )(lg, z, state)


def _row_tile(m):
    return min(m, 512)


def kernel(x_prompt, x_sample, cache_k, cache_v, state_ret, page_table, norm_mix, norm_ffn,
           w_attn_in, q_norm_g, k_norm_g, lambda_q1, lambda_k1, lambda_q2, lambda_k2, subln_g,
           w_attn_out, w_ret_in, w_ret_out, w_up, w_down):
    batch, seq, d = x_prompt.shape
    dec_batch, dec_seq, _ = x_sample.shape
    depth = norm_mix.shape[0]
    half = q_norm_g.shape[1]
    heads_a = d // (2 * half)
    heads_r, dk, dv = state_ret.shape[2:]
    n_pool = cache_k.shape[1]
    assert 2 * half == LANES and cache_k.shape[2] == PAGE_SIZE and dec_seq & (dec_seq - 1) == 0

    xp = x_prompt.reshape(batch * seq, d)
    xs = x_sample.reshape(dec_batch * dec_seq, d)
    tp, ts = _row_tile(xp.shape[0]), _row_tile(xs.shape[0])
    ck = cache_k.reshape(cache_k.shape[0], n_pool, PAGE_SIZE * heads_a, LANES)
    cv = cache_v.reshape(cache_v.shape[0], n_pool, PAGE_SIZE * heads_a, LANES)
    slopes = 2.0 ** (-8.0 * jnp.arange(1, heads_a + 1, dtype=F32) / heads_a) * LOG2E
    log_gamma = jnp.log1p(-(2.0 ** (-5.0 - jnp.arange(heads_r, dtype=F32))))

    kp_l, vp_l, ks_l, vs_l, sp_l, ss_l = [], [], [], [], [], []
    for i in range(depth):
        g_mix = norm_mix[i].reshape(1, d)
        g_ffn = norm_ffn[i].reshape(1, d)
        wu, wd = w_up[i].astype(BF16), w_down[i].astype(BF16)
        if i % 2 == 0:
            a = i // 2
            lam_init = 0.8 - 0.6 * math.exp(-0.3 * i)
            w_in, wo = w_attn_in[a].astype(BF16), w_attn_out[a].astype(BF16)
            qg = jnp.tile(q_norm_g[a], d // half).reshape(1, d)
            kg = jnp.tile(k_norm_g[a], d // half).reshape(1, d)
            sg = subln_g[a].reshape(1, LANES)
            lam4 = jnp.stack([lambda_q1[a], lambda_k1[a], lambda_q2[a], lambda_k2[a]])
            qb, kf, kb, vf, vb = _qkv(xp, g_mix, w_in, qg, kg, tm=tp, half=half)
            qbs, kfs, kbs, vfs, vbs = _qkv(xs, g_mix, w_in, qg, kg, tm=ts, half=half)
            op = _attn_prompt(slopes, lam4, qb, kb, vb, sg, batch=batch, seq=seq, heads=heads_a,
                              half=half, lam_init=lam_init, tq=min(512, seq), tk=min(512, seq))
            os_ = _attn_sample(page_table, slopes, lam4, qbs, kbs, vbs, sg, ck, cv, layer=a,
                               heads=heads_a, dec_seq=dec_seq, half=half, lam_init=lam_init)
            kp_l.append(kf.reshape(batch, seq, heads_a, LANES))
            vp_l.append(vf.reshape(batch, seq, heads_a, LANES))
            ks_l.append(kfs.reshape(dec_batch, dec_seq, heads_a, LANES))
            vs_l.append(vfs.reshape(dec_batch, dec_seq, heads_a, LANES))
        else:
            r = i // 2
            w_in, wo = w_ret_in[r].astype(BF16), w_ret_out[r].astype(BF16)
            zp = _ret_in(xp, g_mix, w_in, tm=tp)
            zs = _ret_in(xs, g_mix, w_in, tm=ts)
            op, sp = _retention_prompt(log_gamma, zp, n_seq=batch, seq=seq, heads=heads_r,
                                       dk=dk, dv=dv)
            zs = jnp.pad(zs.reshape(dec_batch, dec_seq, -1),
                         ((0, 0), (0, SAMPLE_ROWS - dec_seq), (0, 0)))
            os_, ss = _retention_sample(log_gamma, zs, state_ret[r], chunk_len=dec_seq)
            os_ = os_[:, :dec_seq].reshape(dec_batch * dec_seq, -1)
            sp_l.append(sp)
            ss_l.append(ss)
        xp = _proj_mlp(xp, op, wo, g_ffn, wu, wd, tm=tp)
        xs = _proj_mlp(xs, os_, wo, g_ffn, wu, wd, tm=ts)

    return (xp.reshape(batch, seq, d), xs.reshape(dec_batch, dec_seq, d),
            jnp.stack(kp_l), jnp.stack(vp_l), jnp.stack(ks_l), jnp.stack(vs_l),
            jnp.stack(sp_l), jnp.stack(ss_l))
```

```python
import functools
import math

import jax
import jax.numpy as jnp
from jax import lax
from jax.experimental import pallas as pl
from jax.experimental.pallas import tpu as pltpu

F32 = jnp.float32
BF16 = jnp.bfloat16
EPS = 1e-6
NEG = -1e30
LANES = 128
VMEM_LIMIT = 56 * 1024 * 1024
PAGES_PER_STEP = 8
PAGE_SLOTS = 3
RET_CHUNK = 128
PAGE_SIZE = 128
SAMPLE_ROWS = 16
LOG2E = math.log2(math.e)


def _params(sem):
    return pltpu.CompilerParams(dimension_semantics=sem, vmem_limit_bytes=VMEM_LIMIT)


def _resident(shape):
    nd = len(shape)
    return pl.BlockSpec(shape, lambda *_: (0,) * nd, pipeline_mode=pl.Buffered(1))


def _rmsnorm_bf16(x, g):
    return (x * lax.rsqrt(jnp.mean(x * x, axis=-1, keepdims=True) + EPS) * g).astype(BF16)


def _dot(a, b):
    return jnp.dot(a, b, preferred_element_type=F32)


def _dot_nt(a, b):
    return lax.dot_general(a, b, (((1,), (1,)), ((), ())), preferred_element_type=F32)


def _qkv_kernel(x_ref, g_ref, w_ref, qg_ref, kg_ref, qb_ref, kf_ref, kb_ref, vf_ref, vb_ref,
                *, d, half, q_scale):
    h = _rmsnorm_bf16(x_ref[...], g_ref[...])
    tm = h.shape[0]
    n_heads = d // LANES
    lo = lax.broadcasted_iota(jnp.int32, (tm, LANES), 1) < half
    inv = 1.0 / half

    def half_norm(a, gain):
        sq = a * a
        s_lo = jnp.sum(jnp.where(lo, sq, 0.0), axis=-1, keepdims=True)
        s_hi = jnp.sum(jnp.where(lo, 0.0, sq), axis=-1, keepdims=True)
        r = jnp.where(lo, lax.rsqrt(s_lo * inv + EPS), lax.rsqrt(s_hi * inv + EPS))
        return a * r * gain

    q = _dot(h, w_ref[:, 0:d])
    for c in range(d // LANES):
        sl = slice(c * LANES, (c + 1) * LANES)
        qb_ref[:, sl] = (half_norm(q[:, sl], qg_ref[:, sl]) * q_scale).astype(BF16)
    k = _dot(h, w_ref[:, d:2 * d])
    for c in range(d // LANES):
        sl = slice(c * LANES, (c + 1) * LANES)
        kn = half_norm(k[:, sl], kg_ref[:, sl])
        kf_ref[pl.ds(c, tm, stride=n_heads), :] = kn
        kb_ref[:, sl] = kn.astype(BF16)
    v = _dot(h, w_ref[:, 2 * d:3 * d])
    for c in range(n_heads):
        vf_ref[pl.ds(c, tm, stride=n_heads), :] = v[:, c * LANES:(c + 1) * LANES]
    vb_ref[...] = v.astype(BF16)


def _qkv(x, g, w, qg, kg, *, tm, half):
    m, d = x.shape
    row = lambda i: (i, 0)
    n_heads = d // LANES
    bf = jax.ShapeDtypeStruct((m, d), BF16)
    ff = jax.ShapeDtypeStruct((m * n_heads, LANES), F32)
    bf_spec = pl.BlockSpec((tm, d), row)
    ff_spec = pl.BlockSpec((tm * n_heads, LANES), row)
    outs = [bf, ff, bf, ff, bf]
    return pl.pallas_call(
        functools.partial(_qkv_kernel, d=d, half=half, q_scale=half ** -0.5 * LOG2E),
        grid=(m // tm,),
        in_specs=[pl.BlockSpec((tm, d), row), _resident((1, d)), _resident(w.shape),
                  _resident((1, d)), _resident((1, d))],
        out_specs=[bf_spec, ff_spec, bf_spec, ff_spec, bf_spec],
        out_shape=outs,
        compiler_params=_params(("parallel",)),
        name="qkv_proj",
    )(x, g, w, qg, kg)


def _lambda_11(l_ref, lam_init):
    a = jnp.sum(l_ref[0:1, :] * l_ref[1:2, :], axis=-1, keepdims=True)
    b = jnp.sum(l_ref[2:3, :] * l_ref[3:4, :], axis=-1, keepdims=True)
    return jnp.exp(a) - jnp.exp(b) + lam_init


def _subln_bf16(o, sg, lam_init):
    r = lax.rsqrt(jnp.mean(o * o, axis=-1, keepdims=True) + EPS)
    return (o * r * sg * (1.0 - lam_init)).astype(BF16)


def _softmax_step(s, v, m, l, acc):
    m_new = jnp.maximum(m, jnp.max(s, axis=-1, keepdims=True))
    alpha = jnp.exp2(m - m_new)
    p = jnp.exp2(s - m_new)
    l_new = alpha * l + jnp.sum(p, axis=-1, keepdims=True)
    acc_new = alpha * acc + _dot(p.astype(BF16), v)
    return m_new, l_new, acc_new


def _attn_prompt_kernel(slope_ref, lam_ref, q_ref, k_ref, v_ref, sg_ref, mask_ref, o_ref,
                        qt_scr, vt_scr, *, seq, tile, half, lam_init):
    slope = slope_ref[pl.program_id(1)]
    lam = _lambda_11(lam_ref, lam_init)
    extra = vt_scr.shape[0] - LANES
    qt_scr[...] = q_ref[...].astype(F32).T.astype(BF16)
    vt_scr[0:LANES, :] = v_ref[...].astype(F32).T.astype(BF16)
    vt_scr[LANES:, :] = jnp.ones((extra, seq), BF16)
    lo = lax.broadcasted_iota(jnp.int32, (LANES, tile), 0) < half
    zero = jnp.zeros((), BF16)
    lane = lax.broadcasted_iota(jnp.int32, (tile, LANES), 1)
    b0 = slope * lax.broadcasted_iota(jnp.int32, (tile, LANES), 0).astype(F32)
    b_hi = b0.astype(BF16).astype(F32)
    b_mid = (b0 - b_hi).astype(BF16).astype(F32)
    b_lo = b0 - b_hi - b_mid
    k_extra = jnp.where(lane == 0, b_hi, jnp.where(lane == 1, b_mid,
                                                   jnp.where(lane == 2, b_lo, 0.0))).astype(BF16)
    q_extra = jnp.where(lax.broadcasted_iota(jnp.int32, (LANES, 2 * tile), 0) < 3,
                        1.0, 0.0).astype(BF16)

    def query_tile(qi):
        q0 = qi * tile
        qt = qt_scr[:, q0:q0 + tile]
        q2t = jnp.concatenate([jnp.where(lo, qt, zero), jnp.where(lo, zero, qt)], axis=1)
        return jnp.concatenate([q2t, q_extra], axis=0)

    def step(qi, ki, q2t, carry):
        m, acc = carry
        k0, q0 = ki * tile, qi * tile
        s = _dot(jnp.concatenate([k_ref[k0:k0 + tile, :], k_extra], axis=1), q2t)
        if ki == qi:
            s = s + mask_ref[...]
        c = slope * float(k0 - q0)
        m_new = jnp.maximum(m, jnp.max(s, axis=0, keepdims=True) + c)
        alpha = jnp.exp2(m - m_new)
        p = jnp.exp2(s - (m_new - c))
        return m_new, alpha * acc + _dot(vt_scr[:, k0:k0 + tile], p.astype(BF16))

    def finish(qi, carry):
        _, acc = carry
        t = acc[:LANES] / acc[LANES:LANES + 1]
        o = t[:, :tile] - lam * t[:, tile:]
        r = lax.rsqrt(jnp.mean(o * o, axis=0, keepdims=True) + EPS)
        o_ref[qi * tile:(qi + 1) * tile, :] = (
            (o * r).T * sg_ref[...] * (1.0 - lam_init)).astype(BF16)

    n_q = seq // tile
    for pair in range(0, n_q, 2):
        tiles = [qi for qi in (pair + 1, pair) if qi < n_q]
        q2ts = {qi: query_tile(qi) for qi in tiles}
        carries = {qi: (jnp.full((1, 2 * tile), NEG, F32),
                        jnp.zeros((LANES + extra, 2 * tile), F32)) for qi in tiles}
        for ki in range(max(tiles) + 1):
            for qi in tiles:
                if ki <= qi:
                    carries[qi] = step(qi, ki, q2ts[qi], carries[qi])
                    if ki == qi:
                        finish(qi, carries[qi])


def _attn_prompt(slopes, lam4, qb, kb, vb, sg, *, batch, seq, heads, half, lam_init, tile):
    m, d = qb.shape
    blk = pl.BlockSpec((seq, LANES), lambda b, h: (b, h))
    key = jnp.arange(tile)[:, None]
    qry = jnp.arange(2 * tile)[None, :] % tile
    mask = jnp.where(key <= qry, 0.0, NEG).astype(F32)
    return pl.pallas_call(
        functools.partial(_attn_prompt_kernel, seq=seq, tile=tile, half=half, lam_init=lam_init),
        grid=(batch, heads),
        in_specs=[pl.BlockSpec(memory_space=pltpu.SMEM), _resident(lam4.shape),
                  blk, blk, blk, _resident(sg.shape), _resident(mask.shape)],
        out_specs=blk,
        out_shape=jax.ShapeDtypeStruct((m, d), BF16),
        scratch_shapes=[pltpu.VMEM((LANES, seq), BF16), pltpu.VMEM((LANES + 16, seq), BF16)],
        compiler_params=_params(("parallel", "parallel")),
        name="attn_prompt",
    )(slopes, lam4, qb, kb, vb, sg, mask)


def _page_part(s):
    m = jnp.max(s, axis=-1, keepdims=True)
    p = jnp.exp2(s - m)
    return m, jnp.sum(p, axis=-1, keepdims=True), p


def _attn_sample_kernel(pt_ref, lam_ref, q2_ref, kn_ref, vn_ref, sg_ref, mb_ref, mbn_ref, sc_ref,
                        ck_ref, cv_ref, o_ref, k_buf, v_buf, sem, m_scr, l_scr, acc_scr,
                        *, layer, pages, past, lam_init):
    j = pl.program_id(1)
    n_j = pl.num_programs(1)
    step = pl.program_id(0) * n_j + j
    n_steps = pl.num_programs(0) * n_j
    nr = mb_ref.shape[0]

    def page_copies(s):
        slot = s % PAGE_SLOTS
        sb, sj = s // n_j, s % n_j
        out = []
        for i in range(pages):
            page = pt_ref[sb, sj * pages + i]
            out.append(pltpu.make_async_copy(ck_ref.at[layer, page], k_buf.at[slot, i],
                                             sem.at[slot, i]))
            out.append(pltpu.make_async_copy(cv_ref.at[layer, page], v_buf.at[slot, i],
                                             sem.at[slot, pages + i]))
        return out

    @pl.when(step == 0)
    def _():
        for s in range(PAGE_SLOTS - 1):
            for c in page_copies(s):
                c.start()

    @pl.when(step + PAGE_SLOTS - 1 < n_steps)
    def _():
        for c in page_copies(step + PAGE_SLOTS - 1):
            c.start()

    @pl.when(j == 0)
    def _():
        m_scr[...] = jnp.full(m_scr.shape, NEG, F32)
        l_scr[...] = jnp.zeros(l_scr.shape, F32)
        acc_scr[...] = jnp.zeros(acc_scr.shape, F32)

    def merge(parts):
        m_old = m_scr[...]
        m_new = m_old
        for m, _, _ in parts:
            m_new = jnp.maximum(m_new, m)
        a = jnp.exp2(m_old - m_new)
        l = a * l_scr[...]
        acc = a * acc_scr[...]
        for m, lp, op in parts:
            w = jnp.exp2(m - m_new)
            l = l + w * lp
            acc = acc + w * op
        m_scr[...] = m_new
        l_scr[...] = l
        acc_scr[...] = acc

    for c in page_copies(step):
        c.wait()
    slot = step % PAGE_SLOTS
    q2 = q2_ref[...]
    mb = mb_ref[...]
    sc = sc_ref[...]
    parts = []
    for pr in range(pages // 2):
        ia, ib = 2 * pr, 2 * pr + 1
        kcat = jnp.concatenate([k_buf[slot, ia].astype(BF16), k_buf[slot, ib].astype(BF16)], axis=1)
        s2 = _dot_nt(q2, kcat)
        ma, la, pa = _page_part(s2[:nr] + mb)
        mb_, lb, pb = _page_part(s2[nr:] + mb)
        vcat = jnp.concatenate([v_buf[slot, ia].astype(BF16), v_buf[slot, ib].astype(BF16)], axis=1)
        o2 = _dot(jnp.concatenate([pa, pb], axis=0).astype(BF16), vcat)
        off = ((j * pages + ia) * PAGE_SIZE - past).astype(F32)
        parts.append((ma + sc * off, la, o2[:nr, :LANES]))
        parts.append((mb_ + sc * (off + PAGE_SIZE), lb, o2[nr:, LANES:]))
    merge(parts)

    @pl.when(j == n_j - 1)
    def _():
        mn, ln, pn = _page_part(_dot_nt(q2[:nr, :LANES], kn_ref[...]) + mbn_ref[...])
        merge([(mn, ln, _dot(pn.astype(BF16), vn_ref[...]))])
        t = acc_scr[...] / l_scr[...]
        o = t[:nr // 2] - _lambda_11(lam_ref, lam_init) * t[nr // 2:]
        o_ref[...] = _subln_bf16(o, sg_ref[...], lam_init)


def _attn_sample(page_table, slopes, lam4, qbs, kbs, vbs, sg, ck, cv, *, layer, heads, dec_seq,
                 half, lam_init):
    nb = page_table.shape[0]
    d = qbs.shape[1]
    n_pages = page_table.shape[1]
    pages = PAGES_PER_STEP
    past = n_pages * PAGE_SIZE
    hq = heads * dec_seq
    nr = 2 * hq
    assert nb * (n_pages // pages) >= PAGE_SLOTS - 1 and n_pages % pages == 0
    q4 = qbs.reshape(nb, dec_seq, heads, LANES).transpose(0, 2, 1, 3).reshape(nb, hq, LANES)
    lo = jnp.arange(LANES) < half
    zq = jnp.zeros_like(q4)
    qa = jnp.concatenate([jnp.where(lo, q4, zq), jnp.where(lo, zq, q4)], axis=1)
    za = jnp.zeros_like(qa)
    q2 = jnp.concatenate([jnp.concatenate([qa, za], axis=2),
                          jnp.concatenate([za, qa], axis=2)], axis=1)
    n_new = LANES
    pad = ((0, 0), (0, n_new - hq), (0, 0))
    kn = jnp.pad(kbs.reshape(nb, hq, LANES), pad)
    vn = jnp.pad(vbs.reshape(nb, hq, LANES), pad)
    r = jnp.arange(nr)
    r_head, r_query = (r % hq) // dec_seq, r % dec_seq
    r_slope = slopes[r_head][:, None]

    def table(n_cols, causal):
        c = jnp.arange(n_cols)
        c_head, c_tok = c % heads, c // heads
        ok = c_head[None, :] == r_head[:, None]
        if causal:
            ok = ok & (c_tok[None, :] <= r_query[:, None])
        return jnp.where(ok, r_slope * c_tok[None, :].astype(F32), NEG)

    mb = table(PAGE_SIZE * heads, False)
    mbn = table(n_new, True)

    per_b = lambda b, j, pt: (b, 0, 0)
    const2 = lambda b, j, pt: (0, 0)
    page_rows = PAGE_SIZE * heads
    grid_spec = pltpu.PrefetchScalarGridSpec(
        num_scalar_prefetch=1,
        grid=(nb, n_pages // pages),
        in_specs=[pl.BlockSpec(lam4.shape, const2),
                  pl.BlockSpec((None, 2 * nr, 2 * LANES), per_b),
                  pl.BlockSpec((None, n_new, LANES), per_b),
                  pl.BlockSpec((None, n_new, LANES), per_b),
                  pl.BlockSpec(sg.shape, const2),
                  pl.BlockSpec(mb.shape, const2),
                  pl.BlockSpec(mbn.shape, const2),
                  pl.BlockSpec(r_slope.shape, const2),
                  pl.BlockSpec(memory_space=pl.ANY),
                  pl.BlockSpec(memory_space=pl.ANY)],
        out_specs=pl.BlockSpec((None, hq, LANES), per_b),
        scratch_shapes=[pltpu.VMEM((PAGE_SLOTS, pages, page_rows, LANES), F32),
                        pltpu.VMEM((PAGE_SLOTS, pages, page_rows, LANES), F32),
                        pltpu.SemaphoreType.DMA((PAGE_SLOTS, 2 * pages)),
                        pltpu.VMEM((nr, 1), F32), pltpu.VMEM((nr, 1), F32),
                        pltpu.VMEM((nr, LANES), F32)],
    )
    out = pl.pallas_call(
        functools.partial(_attn_sample_kernel, layer=layer, pages=pages, past=past,
                          lam_init=lam_init),
        grid_spec=grid_spec,
        out_shape=jax.ShapeDtypeStruct((nb, hq, LANES), BF16),
        compiler_params=_params(("arbitrary", "arbitrary")),
        name="attn_sample",
    )(page_table, lam4, q2, kn, vn, sg, mb, mbn, r_slope, ck, cv)
    return out.reshape(nb, heads, dec_seq, LANES).transpose(0, 2, 1, 3).reshape(nb * dec_seq, d)


def _proj_mlp_kernel(x_ref, a_ref, wo_ref, g_ref, wu_ref, wd_ref, o_ref, *, ff_chunk):
    x1 = x_ref[...] + _dot(a_ref[...], wo_ref[...])
    h = _rmsnorm_bf16(x1, g_ref[...])
    acc = x1
    for c in range(wu_ref.shape[1] // ff_chunk):
        sl = slice(c * ff_chunk, (c + 1) * ff_chunk)
        u = jnp.maximum(_dot(h, wu_ref[:, sl]), 0.0)
        acc = acc + _dot((u * u).astype(BF16), wd_ref[sl, :])
    o_ref[...] = acc


def _proj_mlp(x, a, wo, g, wu, wd, *, tm):
    m, d = x.shape
    row = lambda i: (i, 0)
    return pl.pallas_call(
        functools.partial(_proj_mlp_kernel, ff_chunk=min(1024, wu.shape[1])),
        grid=(m // tm,),
        in_specs=[pl.BlockSpec((tm, d), row), pl.BlockSpec((tm, a.shape[1]), row),
                  _resident(wo.shape), _resident((1, d)), _resident(wu.shape), _resident(wd.shape)],
        out_specs=pl.BlockSpec((tm, d), row),
        out_shape=jax.ShapeDtypeStruct((m, d), F32),
        compiler_params=_params(("parallel",)),
        name="proj_mlp",
    )(x, a, wo, g, wu, wd)


def _ret_in_kernel(x_ref, g_ref, w_ref, z_ref, *, n_chunk):
    h = _rmsnorm_bf16(x_ref[...], g_ref[...])
    for c in range(w_ref.shape[1] // n_chunk):
        sl = slice(c * n_chunk, (c + 1) * n_chunk)
        z_ref[:, sl] = _dot(h, w_ref[:, sl]).astype(BF16)


def _ret_in(x, g, w, *, tm):
    m, d = x.shape
    n = w.shape[1]
    row = lambda i: (i, 0)
    return pl.pallas_call(
        functools.partial(_ret_in_kernel, n_chunk=min(1024, n)),
        grid=(m // tm,),
        in_specs=[pl.BlockSpec((tm, d), row), _resident((1, d)), _resident(w.shape)],
        out_specs=pl.BlockSpec((tm, n), row),
        out_shape=jax.ShapeDtypeStruct((m, n), BF16),
        compiler_params=_params(("parallel",)),
        name="ret_in_proj",
    )(x, g, w)


def _ret_tables(lg, chunk_len, k_scale):
    c = RET_CHUNK
    idx = lax.broadcasted_iota(jnp.int32, (c, 1), 0).astype(F32)
    diff = (lax.broadcasted_iota(jnp.int32, (c, c), 0)
            - lax.broadcasted_iota(jnp.int32, (c, c), 1)).astype(F32)
    dmat = jnp.where(diff >= 0, jnp.exp(lg * jnp.maximum(diff, 0.0)), 0.0) * k_scale
    q_dec = jnp.exp(lg * (idx + 1.0))
    k_dec = jnp.exp(lg * (chunk_len - 1.0 - idx)) * k_scale
    s_dec = jnp.exp(jnp.full((1, 1), lg * chunk_len, F32))
    return dmat, q_dec, k_dec, s_dec


def _ret_chunk(q, k, v, state, tables):
    dmat, q_dec, k_dec, s_dec = tables
    qk = _dot_nt(q, k) * dmat
    o = _dot(qk.astype(BF16), v) + q_dec * _dot(q, state.astype(BF16))
    kd_t = (k.astype(F32) * k_dec).T.astype(BF16)
    return o, s_dec * state + _dot(kd_t, v)


def _ret_gate(o, g):
    y = o * lax.rsqrt(jnp.mean(o * o, axis=-1, keepdims=True) + EPS)
    g = g.astype(F32)
    return (g * jax.nn.sigmoid(g) * y).astype(BF16)


def _retention_prompt_kernel(lg_ref, q_ref, k_ref, v_ref, g_ref, y_ref, s_ref, *, n_chunks,
                             k_scale):
    c = RET_CHUNK
    tables = _ret_tables(lg_ref[pl.program_id(1)], c, k_scale)
    s_ref[...] = jnp.zeros(s_ref.shape, F32)

    def chunk(i, _):
        rows = pl.ds(pl.multiple_of(i * c, c), c)
        o, s_ref[...] = _ret_chunk(q_ref[rows, :], k_ref[rows, :], v_ref[rows, :], s_ref[...],
                                   tables)
        y_ref[rows, :] = _ret_gate(o, g_ref[rows, :])
        return 0

    lax.fori_loop(0, n_chunks, chunk, 0, unroll=2)


def _retention_prompt(lg, z, *, n_seq, seq, heads, dk, dv):
    v_off = 2 * heads * dk // dv
    in_specs = [pl.BlockSpec(memory_space=pltpu.SMEM),
                pl.BlockSpec((seq, dk), lambda b, h: (b, h)),
                pl.BlockSpec((seq, dk), lambda b, h: (b, heads + h)),
                pl.BlockSpec((seq, dv), lambda b, h: (b, v_off + h)),
                pl.BlockSpec((seq, dv), lambda b, h: (b, v_off + heads + h))]
    return pl.pallas_call(
        functools.partial(_retention_prompt_kernel, n_chunks=seq // RET_CHUNK, k_scale=dk ** -0.5),
        grid=(n_seq, heads),
        in_specs=in_specs,
        out_specs=[pl.BlockSpec((seq, dv), lambda b, h: (b, h)),
                   pl.BlockSpec((None, None, dk, dv), lambda b, h: (b, h, 0, 0))],
        out_shape=[jax.ShapeDtypeStruct((z.shape[0], heads * dv), BF16),
                   jax.ShapeDtypeStruct((n_seq, heads, dk, dv), F32)],
        compiler_params=_params(("parallel", "parallel")),
        name="retention_prompt",
    )(lg, z, z, z, z)


def _retention_sample_kernel(lg_ref, z_ref, s0_ref, y_ref, s_ref, *, heads, dk, dv, chunk_len,
                             k_scale):
    rows = z_ref.shape[0]

    def padded(lo, width):
        a = z_ref[:, lo:lo + width]
        return jnp.concatenate([a, jnp.zeros((RET_CHUNK - rows, width), a.dtype)], axis=0)

    for h in range(heads):
        tables = _ret_tables(lg_ref[h], chunk_len, k_scale)
        v_lo = 2 * heads * dk + h * dv
        o, s_ref[h] = _ret_chunk(padded(h * dk, dk), padded((heads + h) * dk, dk),
                                 padded(v_lo, dv), s0_ref[h], tables)
        y_ref[:, h * dv:(h + 1) * dv] = _ret_gate(
            o[:rows], z_ref[:, v_lo + heads * dv:v_lo + (heads + 1) * dv])


def _retention_sample(lg, z, state, *, chunk_len):
    n_seq, rows, n = z.shape
    _, heads, dk, dv = state.shape
    per_b3 = lambda b: (b, 0, 0)
    per_b4 = lambda b: (b, 0, 0, 0)
    return pl.pallas_call(
        functools.partial(_retention_sample_kernel, heads=heads, dk=dk, dv=dv, chunk_len=chunk_len,
                          k_scale=dk ** -0.5),
        grid=(n_seq,),
        in_specs=[pl.BlockSpec(memory_space=pltpu.SMEM),
                  pl.BlockSpec((None, rows, n), per_b3),
                  pl.BlockSpec((None, heads, dk, dv), per_b4)],
        out_specs=[pl.BlockSpec((None, rows, heads * dv), per_b3),
                   pl.BlockSpec((None, heads, dk, dv), per_b4)],
        out_shape=[jax.ShapeDtypeStruct((n_seq, rows, heads * dv), BF16),
                   jax.ShapeDtypeStruct(state.shape, F32)],
        compiler_params=_params(("parallel",)),
        name="retention_sample",
    )(lg, z, state)


def _row_tile(m):
    return min(m, 512)


def kernel(x_prompt, x_sample, cache_k, cache_v, state_ret, page_table, norm_mix, norm_ffn,
           w_attn_in, q_norm_g, k_norm_g, lambda_q1, lambda_k1, lambda_q2, lambda_k2, subln_g,
           w_attn_out, w_ret_in, w_ret_out, w_up, w_down):
    batch, seq, d = x_prompt.shape
    dec_batch, dec_seq, _ = x_sample.shape
    depth = norm_mix.shape[0]
    half = q_norm_g.shape[1]
    heads_a = d // (2 * half)
    heads_r, dk, dv = state_ret.shape[2:]
    n_pool = cache_k.shape[1]
    assert 2 * half == LANES and cache_k.shape[2] == PAGE_SIZE and dec_seq & (dec_seq - 1) == 0

    xp = x_prompt.reshape(batch * seq, d)
    xs = x_sample.reshape(dec_batch * dec_seq, d)
    tp, ts = _row_tile(xp.shape[0]), _row_tile(xs.shape[0])
    ck = cache_k.reshape(cache_k.shape[0], n_pool, PAGE_SIZE * heads_a, LANES)
    cv = cache_v.reshape(cache_v.shape[0], n_pool, PAGE_SIZE * heads_a, LANES)
    slopes = 2.0 ** (-8.0 * jnp.arange(1, heads_a + 1, dtype=F32) / heads_a) * LOG2E
    log_gamma = jnp.log1p(-(2.0 ** (-5.0 - jnp.arange(heads_r, dtype=F32))))

    kp_l, vp_l, ks_l, vs_l, sp_l, ss_l = [], [], [], [], [], []
    for i in range(depth):
        g_mix = norm_mix[i].reshape(1, d)
        g_ffn = norm_ffn[i].reshape(1, d)
        wu, wd = w_up[i].astype(BF16), w_down[i].astype(BF16)
        if i % 2 == 0:
            a = i // 2
            lam_init = 0.8 - 0.6 * math.exp(-0.3 * i)
            w_in, wo = w_attn_in[a].astype(BF16), w_attn_out[a].astype(BF16)
            qg = jnp.tile(q_norm_g[a], d // half).reshape(1, d)
            kg = jnp.tile(k_norm_g[a], d // half).reshape(1, d)
            sg = subln_g[a].reshape(1, LANES)
            lam4 = jnp.stack([lambda_q1[a], lambda_k1[a], lambda_q2[a], lambda_k2[a]])
            qb, kf, kb, vf, vb = _qkv(xp, g_mix, w_in, qg, kg, tm=tp, half=half)
            qbs, kfs, kbs, vfs, vbs = _qkv(xs, g_mix, w_in, qg, kg, tm=ts, half=half)
            op = _attn_prompt(slopes, lam4, qb, kb, vb, sg, batch=batch, seq=seq, heads=heads_a,
                              half=half, lam_init=lam_init, tile=min(512, seq))
            os_ = _attn_sample(page_table, slopes, lam4, qbs, kbs, vbs, sg, ck, cv, layer=a,
                               heads=heads_a, dec_seq=dec_seq, half=half, lam_init=lam_init)
            kp_l.append(kf.reshape(batch, seq, heads_a, LANES))
            vp_l.append(vf.reshape(batch, seq, heads_a, LANES))
            ks_l.append(kfs.reshape(dec_batch, dec_seq, heads_a, LANES))
            vs_l.append(vfs.reshape(dec_batch, dec_seq, heads_a, LANES))
        else:
            r = i // 2
            w_in, wo = w_ret_in[r].astype(BF16), w_ret_out[r].astype(BF16)
            zp = _ret_in(xp, g_mix, w_in, tm=tp)
            zs = _ret_in(xs, g_mix, w_in, tm=ts)
            op, sp = _retention_prompt(log_gamma, zp, n_seq=batch, seq=seq, heads=heads_r,
                                       dk=dk, dv=dv)
            zs = jnp.pad(zs.reshape(dec_batch, dec_seq, -1),
                         ((0, 0), (0, SAMPLE_ROWS - dec_seq), (0, 0)))
            os_, ss = _retention_sample(log_gamma, zs, state_ret[r], chunk_len=dec_seq)
            os_ = os_[:, :dec_seq].reshape(dec_batch * dec_seq, -1)
            sp_l.append(sp)
            ss_l.append(ss)
        xp = _proj_mlp(xp, op, wo, g_ffn, wu, wd, tm=tp)
        xs = _proj_mlp(xs, os_, wo, g_ffn, wu, wd, tm=ts)

    return (xp.reshape(batch, seq, d), xs.reshape(dec_batch, dec_seq, d),
            jnp.stack(kp_l), jnp.stack(vp_l), jnp.stack(ks_l), jnp.stack(vs_l),
            jnp.stack(sp_l), jnp.stack(ss_l))
```

```python
import functools
import math

import jax
import jax.numpy as jnp
from jax import lax
from jax.experimental import pallas as pl
from jax.experimental.pallas import tpu as pltpu

F32 = jnp.float32
BF16 = jnp.bfloat16
EPS = 1e-6
NEG = -1e30
LANES = 128
VMEM_LIMIT = 56 * 1024 * 1024
PAGES_PER_STEP = 8
PAGE_SLOTS = 3
RET_CHUNK = 128
PAGE_SIZE = 128
ATTN_KEY_STEP = 256
SAMPLE_ROWS = 16
LOG2E = math.log2(math.e)


def _params(sem):
    return pltpu.CompilerParams(dimension_semantics=sem, vmem_limit_bytes=VMEM_LIMIT)


def _resident(shape):
    nd = len(shape)
    return pl.BlockSpec(shape, lambda *_: (0,) * nd, pipeline_mode=pl.Buffered(1))


def _rmsnorm_bf16(x, g):
    return (x * lax.rsqrt(jnp.mean(x * x, axis=-1, keepdims=True) + EPS) * g).astype(BF16)


def _dot(a, b):
    return jnp.dot(a, b, preferred_element_type=F32)


def _dot_nt(a, b):
    return lax.dot_general(a, b, (((1,), (1,)), ((), ())), preferred_element_type=F32)


def _qkv_kernel(x_ref, g_ref, w_ref, qg_ref, kg_ref, qb_ref, kf_ref, kb_ref, vf_ref, vb_ref,
                *, d, half, q_scale):
    h = _rmsnorm_bf16(x_ref[...], g_ref[...])
    tm = h.shape[0]
    n_heads = d // LANES
    lo = lax.broadcasted_iota(jnp.int32, (tm, LANES), 1) < half
    inv = 1.0 / half

    def half_norm(a, gain):
        sq = a * a
        s_lo = jnp.sum(jnp.where(lo, sq, 0.0), axis=-1, keepdims=True)
        s_hi = jnp.sum(jnp.where(lo, 0.0, sq), axis=-1, keepdims=True)
        r = jnp.where(lo, lax.rsqrt(s_lo * inv + EPS), lax.rsqrt(s_hi * inv + EPS))
        return a * r * gain

    q = _dot(h, w_ref[:, 0:d])
    for c in range(d // LANES):
        sl = slice(c * LANES, (c + 1) * LANES)
        qb_ref[:, sl] = (half_norm(q[:, sl], qg_ref[:, sl]) * q_scale).astype(BF16)
    k = _dot(h, w_ref[:, d:2 * d])
    for c in range(d // LANES):
        sl = slice(c * LANES, (c + 1) * LANES)
        kn = half_norm(k[:, sl], kg_ref[:, sl])
        kf_ref[pl.ds(c, tm, stride=n_heads), :] = kn
        kb_ref[:, sl] = kn.astype(BF16)
    v = _dot(h, w_ref[:, 2 * d:3 * d])
    for c in range(n_heads):
        vf_ref[pl.ds(c, tm, stride=n_heads), :] = v[:, c * LANES:(c + 1) * LANES]
    vb_ref[...] = v.astype(BF16)


def _qkv(x, g, w, qg, kg, *, tm, half):
    m, d = x.shape
    row = lambda i: (i, 0)
    n_heads = d // LANES
    bf = jax.ShapeDtypeStruct((m, d), BF16)
    ff = jax.ShapeDtypeStruct((m * n_heads, LANES), F32)
    bf_spec = pl.BlockSpec((tm, d), row)
    ff_spec = pl.BlockSpec((tm * n_heads, LANES), row)
    outs = [bf, ff, bf, ff, bf]
    return pl.pallas_call(
        functools.partial(_qkv_kernel, d=d, half=half, q_scale=half ** -0.5 * LOG2E),
        grid=(m // tm,),
        in_specs=[pl.BlockSpec((tm, d), row), _resident((1, d)), _resident(w.shape),
                  _resident((1, d)), _resident((1, d))],
        out_specs=[bf_spec, ff_spec, bf_spec, ff_spec, bf_spec],
        out_shape=outs,
        compiler_params=_params(("parallel",)),
        name="qkv_proj",
    )(x, g, w, qg, kg)


def _lambda_11(l_ref, lam_init):
    a = jnp.sum(l_ref[0:1, :] * l_ref[1:2, :], axis=-1, keepdims=True)
    b = jnp.sum(l_ref[2:3, :] * l_ref[3:4, :], axis=-1, keepdims=True)
    return jnp.exp(a) - jnp.exp(b) + lam_init


def _subln_bf16(o, sg, lam_init):
    r = lax.rsqrt(jnp.mean(o * o, axis=-1, keepdims=True) + EPS)
    return (o * r * sg * (1.0 - lam_init)).astype(BF16)


def _softmax_step(s, v, m, l, acc):
    m_new = jnp.maximum(m, jnp.max(s, axis=-1, keepdims=True))
    alpha = jnp.exp2(m - m_new)
    p = jnp.exp2(s - m_new)
    l_new = alpha * l + jnp.sum(p, axis=-1, keepdims=True)
    acc_new = alpha * acc + _dot(p.astype(BF16), v)
    return m_new, l_new, acc_new


def _attn_prompt_kernel(slope_ref, lam_ref, q_ref, k_ref, v_ref, sg_ref, mask_ref, o_ref,
                        qt_scr, vt_scr, *, seq, tile, kstep, half, lam_init):
    slope = slope_ref[pl.program_id(1)]
    lam = _lambda_11(lam_ref, lam_init)
    extra = vt_scr.shape[0] - LANES
    qt_scr[...] = q_ref[...].astype(F32).T.astype(BF16)
    vt_scr[0:LANES, :] = v_ref[...].astype(F32).T.astype(BF16)
    vt_scr[LANES:, :] = jnp.ones((extra, seq), BF16)
    lo = lax.broadcasted_iota(jnp.int32, (LANES, tile), 0) < half
    zero = jnp.zeros((), BF16)
    lane = lax.broadcasted_iota(jnp.int32, (kstep, LANES), 1)
    b0 = slope * lax.broadcasted_iota(jnp.int32, (kstep, LANES), 0).astype(F32)
    b_hi = b0.astype(BF16).astype(F32)
    b_mid = (b0 - b_hi).astype(BF16).astype(F32)
    b_lo = b0 - b_hi - b_mid
    k_extra = jnp.where(lane == 0, b_hi, jnp.where(lane == 1, b_mid,
                                                   jnp.where(lane == 2, b_lo, 0.0))).astype(BF16)
    q_extra = jnp.where(lax.broadcasted_iota(jnp.int32, (LANES, 2 * tile), 0) < 3,
                        1.0, 0.0).astype(BF16)

    for qi in range(seq // tile):
        q0 = qi * tile
        qt = qt_scr[:, q0:q0 + tile]
        c1, c2 = jnp.where(lo, qt, zero), jnp.where(lo, zero, qt)
        q2t = jnp.concatenate(
            [c[:, j:j + kstep] for j in range(0, tile, kstep) for c in (c1, c2)], axis=1)
        q2t = jnp.concatenate([q2t, q_extra], axis=0)
        m = jnp.full((1, 2 * tile), NEG, F32)
        acc = jnp.zeros((LANES + extra, 2 * tile), F32)
        for k0 in range(0, q0 + tile, kstep):
            first = 2 * max(k0 - q0, 0)
            s = _dot(jnp.concatenate([k_ref[k0:k0 + kstep, :], k_extra], axis=1), q2t[:, first:])
            if k0 >= q0:
                blk = s[:, :2 * kstep] + mask_ref[...]
                s = blk if first + 2 * kstep == 2 * tile else jnp.concatenate(
                    [blk, s[:, 2 * kstep:]], axis=1)
            c = slope * float(k0 - q0)
            m_old = m[:, first:]
            m_new = jnp.maximum(m_old, jnp.max(s, axis=0, keepdims=True) + c)
            alpha = jnp.exp2(m_old - m_new)
            p = jnp.exp2(s - (m_new - c))
            acc_new = alpha * acc[:, first:] + _dot(vt_scr[:, k0:k0 + kstep], p.astype(BF16))
            if first:
                m_new = jnp.concatenate([m[:, :first], m_new], axis=1)
                acc_new = jnp.concatenate([acc[:, :first], acc_new], axis=1)
            m, acc = m_new, acc_new
        t = acc[:LANES] / acc[LANES:LANES + 1]
        o = [t[:, j:j + kstep] - lam * t[:, j + kstep:j + 2 * kstep]
             for j in range(0, 2 * tile, 2 * kstep)]
        o = o[0] if len(o) == 1 else jnp.concatenate(o, axis=1)
        r = lax.rsqrt(jnp.mean(o * o, axis=0, keepdims=True) + EPS)
        o_ref[q0:q0 + tile, :] = ((o * r).T * sg_ref[...] * (1.0 - lam_init)).astype(BF16)


def _attn_prompt(slopes, lam4, qb, kb, vb, sg, *, batch, seq, heads, half, lam_init, tile):
    m, d = qb.shape
    blk = pl.BlockSpec((seq, LANES), lambda b, h: (b, h))
    kstep = min(ATTN_KEY_STEP, tile)
    key = jnp.arange(kstep)[:, None]
    qry = jnp.arange(2 * kstep)[None, :] % kstep
    mask = jnp.where(key <= qry, 0.0, NEG).astype(F32)
    return pl.pallas_call(
        functools.partial(_attn_prompt_kernel, seq=seq, tile=tile, kstep=kstep,
                          half=half, lam_init=lam_init),
        grid=(batch, heads),
        in_specs=[pl.BlockSpec(memory_space=pltpu.SMEM), _resident(lam4.shape),
                  blk, blk, blk, _resident(sg.shape), _resident(mask.shape)],
        out_specs=blk,
        out_shape=jax.ShapeDtypeStruct((m, d), BF16),
        scratch_shapes=[pltpu.VMEM((LANES, seq), BF16), pltpu.VMEM((LANES + 16, seq), BF16)],
        compiler_params=_params(("parallel", "parallel")),
        name="attn_prompt",
    )(slopes, lam4, qb, kb, vb, sg, mask)


def _page_part(s):
    m = jnp.max(s, axis=-1, keepdims=True)
    p = jnp.exp2(s - m)
    return m, jnp.sum(p, axis=-1, keepdims=True), p


def _attn_sample_kernel(pt_ref, lam_ref, q2_ref, kn_ref, vn_ref, sg_ref, mb_ref, mbn_ref, sc_ref,
                        ck_ref, cv_ref, o_ref, k_buf, v_buf, sem, m_scr, l_scr, acc_scr,
                        *, layer, pages, past, lam_init):
    j = pl.program_id(1)
    n_j = pl.num_programs(1)
    step = pl.program_id(0) * n_j + j
    n_steps = pl.num_programs(0) * n_j
    nr = mb_ref.shape[0]

    def page_copies(s):
        slot = s % PAGE_SLOTS
        sb, sj = s // n_j, s % n_j
        out = []
        for i in range(pages):
            page = pt_ref[sb, sj * pages + i]
            out.append(pltpu.make_async_copy(ck_ref.at[layer, page], k_buf.at[slot, i],
                                             sem.at[slot, i]))
            out.append(pltpu.make_async_copy(cv_ref.at[layer, page], v_buf.at[slot, i],
                                             sem.at[slot, pages + i]))
        return out

    @pl.when(step == 0)
    def _():
        for s in range(PAGE_SLOTS - 1):
            for c in page_copies(s):
                c.start()

    @pl.when(step + PAGE_SLOTS - 1 < n_steps)
    def _():
        for c in page_copies(step + PAGE_SLOTS - 1):
            c.start()

    @pl.when(j == 0)
    def _():
        m_scr[...] = jnp.full(m_scr.shape, NEG, F32)
        l_scr[...] = jnp.zeros(l_scr.shape, F32)
        acc_scr[...] = jnp.zeros(acc_scr.shape, F32)

    def merge(parts):
        m_old = m_scr[...]
        m_new = m_old
        for m, _, _ in parts:
            m_new = jnp.maximum(m_new, m)
        a = jnp.exp2(m_old - m_new)
        l = a * l_scr[...]
        acc = a * acc_scr[...]
        for m, lp, op in parts:
            w = jnp.exp2(m - m_new)
            l = l + w * lp
            acc = acc + w * op
        m_scr[...] = m_new
        l_scr[...] = l
        acc_scr[...] = acc

    for c in page_copies(step):
        c.wait()
    slot = step % PAGE_SLOTS
    q2 = q2_ref[...]
    mb = mb_ref[...]
    sc = sc_ref[...]
    parts = []
    for pr in range(pages // 2):
        ia, ib = 2 * pr, 2 * pr + 1
        kcat = jnp.concatenate([k_buf[slot, ia].astype(BF16), k_buf[slot, ib].astype(BF16)], axis=1)
        s2 = _dot_nt(q2, kcat)
        ma, la, pa = _page_part(s2[:nr] + mb)
        mb_, lb, pb = _page_part(s2[nr:] + mb)
        vcat = jnp.concatenate([v_buf[slot, ia].astype(BF16), v_buf[slot, ib].astype(BF16)], axis=1)
        o2 = _dot(jnp.concatenate([pa, pb], axis=0).astype(BF16), vcat)
        off = ((j * pages + ia) * PAGE_SIZE - past).astype(F32)
        parts.append((ma + sc * off, la, o2[:nr, :LANES]))
        parts.append((mb_ + sc * (off + PAGE_SIZE), lb, o2[nr:, LANES:]))
    merge(parts)

    @pl.when(j == n_j - 1)
    def _():
        mn, ln, pn = _page_part(_dot_nt(q2[:nr, :LANES], kn_ref[...]) + mbn_ref[...])
        merge([(mn, ln, _dot(pn.astype(BF16), vn_ref[...]))])
        t = acc_scr[...] / l_scr[...]
        o = t[:nr // 2] - _lambda_11(lam_ref, lam_init) * t[nr // 2:]
        o_ref[...] = _subln_bf16(o, sg_ref[...], lam_init)


def _attn_sample(page_table, slopes, lam4, qbs, kbs, vbs, sg, ck, cv, *, layer, heads, dec_seq,
                 half, lam_init):
    nb = page_table.shape[0]
    d = qbs.shape[1]
    n_pages = page_table.shape[1]
    pages = PAGES_PER_STEP
    past = n_pages * PAGE_SIZE
    hq = heads * dec_seq
    nr = 2 * hq
    assert nb * (n_pages // pages) >= PAGE_SLOTS - 1 and n_pages % pages == 0
    q4 = qbs.reshape(nb, dec_seq, heads, LANES).transpose(0, 2, 1, 3).reshape(nb, hq, LANES)
    lo = jnp.arange(LANES) < half
    zq = jnp.zeros_like(q4)
    qa = jnp.concatenate([jnp.where(lo, q4, zq), jnp.where(lo, zq, q4)], axis=1)
    za = jnp.zeros_like(qa)
    q2 = jnp.concatenate([jnp.concatenate([qa, za], axis=2),
                          jnp.concatenate([za, qa], axis=2)], axis=1)
    n_new = LANES
    pad = ((0, 0), (0, n_new - hq), (0, 0))
    kn = jnp.pad(kbs.reshape(nb, hq, LANES), pad)
    vn = jnp.pad(vbs.reshape(nb, hq, LANES), pad)
    r = jnp.arange(nr)
    r_head, r_query = (r % hq) // dec_seq, r % dec_seq
    r_slope = slopes[r_head][:, None]

    def table(n_cols, causal):
        c = jnp.arange(n_cols)
        c_head, c_tok = c % heads, c // heads
        ok = c_head[None, :] == r_head[:, None]
        if causal:
            ok = ok & (c_tok[None, :] <= r_query[:, None])
        return jnp.where(ok, r_slope * c_tok[None, :].astype(F32), NEG)

    mb = table(PAGE_SIZE * heads, False)
    mbn = table(n_new, True)

    per_b = lambda b, j, pt: (b, 0, 0)
    const2 = lambda b, j, pt: (0, 0)
    page_rows = PAGE_SIZE * heads
    grid_spec = pltpu.PrefetchScalarGridSpec(
        num_scalar_prefetch=1,
        grid=(nb, n_pages // pages),
        in_specs=[pl.BlockSpec(lam4.shape, const2),
                  pl.BlockSpec((None, 2 * nr, 2 * LANES), per_b),
                  pl.BlockSpec((None, n_new, LANES), per_b),
                  pl.BlockSpec((None, n_new, LANES), per_b),
                  pl.BlockSpec(sg.shape, const2),
                  pl.BlockSpec(mb.shape, const2),
                  pl.BlockSpec(mbn.shape, const2),
                  pl.BlockSpec(r_slope.shape, const2),
                  pl.BlockSpec(memory_space=pl.ANY),
                  pl.BlockSpec(memory_space=pl.ANY)],
        out_specs=pl.BlockSpec((None, hq, LANES), per_b),
        scratch_shapes=[pltpu.VMEM((PAGE_SLOTS, pages, page_rows, LANES), F32),
                        pltpu.VMEM((PAGE_SLOTS, pages, page_rows, LANES), F32),
                        pltpu.SemaphoreType.DMA((PAGE_SLOTS, 2 * pages)),
                        pltpu.VMEM((nr, 1), F32), pltpu.VMEM((nr, 1), F32),
                        pltpu.VMEM((nr, LANES), F32)],
    )
    out = pl.pallas_call(
        functools.partial(_attn_sample_kernel, layer=layer, pages=pages, past=past,
                          lam_init=lam_init),
        grid_spec=grid_spec,
        out_shape=jax.ShapeDtypeStruct((nb, hq, LANES), BF16),
        compiler_params=_params(("arbitrary", "arbitrary")),
        name="attn_sample",
    )(page_table, lam4, q2, kn, vn, sg, mb, mbn, r_slope, ck, cv)
    return out.reshape(nb, heads, dec_seq, LANES).transpose(0, 2, 1, 3).reshape(nb * dec_seq, d)


def _proj_mlp_kernel(x_ref, a_ref, wo_ref, g_ref, wu_ref, wd_ref, o_ref, *, ff_chunk):
    x1 = x_ref[...] + _dot(a_ref[...], wo_ref[...])
    h = _rmsnorm_bf16(x1, g_ref[...])
    acc = x1
    for c in range(wu_ref.shape[1] // ff_chunk):
        sl = slice(c * ff_chunk, (c + 1) * ff_chunk)
        u = jnp.maximum(_dot(h, wu_ref[:, sl]), 0.0)
        acc = acc + _dot((u * u).astype(BF16), wd_ref[sl, :])
    o_ref[...] = acc


def _proj_mlp(x, a, wo, g, wu, wd, *, tm):
    m, d = x.shape
    row = lambda i: (i, 0)
    return pl.pallas_call(
        functools.partial(_proj_mlp_kernel, ff_chunk=min(1024, wu.shape[1])),
        grid=(m // tm,),
        in_specs=[pl.BlockSpec((tm, d), row), pl.BlockSpec((tm, a.shape[1]), row),
                  _resident(wo.shape), _resident((1, d)), _resident(wu.shape), _resident(wd.shape)],
        out_specs=pl.BlockSpec((tm, d), row),
        out_shape=jax.ShapeDtypeStruct((m, d), F32),
        compiler_params=_params(("parallel",)),
        name="proj_mlp",
    )(x, a, wo, g, wu, wd)


def _ret_in_kernel(x_ref, g_ref, w_ref, z_ref, *, n_chunk):
    h = _rmsnorm_bf16(x_ref[...], g_ref[...])
    for c in range(w_ref.shape[1] // n_chunk):
        sl = slice(c * n_chunk, (c + 1) * n_chunk)
        z_ref[:, sl] = _dot(h, w_ref[:, sl]).astype(BF16)


def _ret_in(x, g, w, *, tm):
    m, d = x.shape
    n = w.shape[1]
    row = lambda i: (i, 0)
    return pl.pallas_call(
        functools.partial(_ret_in_kernel, n_chunk=min(1024, n)),
        grid=(m // tm,),
        in_specs=[pl.BlockSpec((tm, d), row), _resident((1, d)), _resident(w.shape)],
        out_specs=pl.BlockSpec((tm, n), row),
        out_shape=jax.ShapeDtypeStruct((m, n), BF16),
        compiler_params=_params(("parallel",)),
        name="ret_in_proj",
    )(x, g, w)


def _ret_tables(lg, chunk_len, k_scale):
    c = RET_CHUNK
    idx = lax.broadcasted_iota(jnp.int32, (c, 1), 0).astype(F32)
    diff = (lax.broadcasted_iota(jnp.int32, (c, c), 0)
            - lax.broadcasted_iota(jnp.int32, (c, c), 1)).astype(F32)
    dmat = jnp.where(diff >= 0, jnp.exp(lg * jnp.maximum(diff, 0.0)), 0.0) * k_scale
    q_dec = jnp.exp(lg * (idx + 1.0))
    k_dec = jnp.exp(lg * (chunk_len - 1.0 - idx)) * k_scale
    s_dec = jnp.exp(jnp.full((1, 1), lg * chunk_len, F32))
    return dmat, q_dec, k_dec, s_dec


def _ret_chunk(q, k, v, state, tables):
    dmat, q_dec, k_dec, s_dec = tables
    qk = _dot_nt(q, k) * dmat
    o = _dot(qk.astype(BF16), v) + q_dec * _dot(q, state.astype(BF16))
    kd_t = (k.astype(F32) * k_dec).T.astype(BF16)
    return o, s_dec * state + _dot(kd_t, v)


def _ret_gate(o, g):
    y = o * lax.rsqrt(jnp.mean(o * o, axis=-1, keepdims=True) + EPS)
    g = g.astype(F32)
    return (g * jax.nn.sigmoid(g) * y).astype(BF16)


def _retention_prompt_kernel(lg_ref, q_ref, k_ref, v_ref, g_ref, y_ref, s_ref, *, n_chunks,
                             k_scale):
    c = RET_CHUNK
    tables = _ret_tables(lg_ref[pl.program_id(1)], c, k_scale)
    s_ref[...] = jnp.zeros(s_ref.shape, F32)

    def chunk(i, _):
        rows = pl.ds(pl.multiple_of(i * c, c), c)
        o, s_ref[...] = _ret_chunk(q_ref[rows, :], k_ref[rows, :], v_ref[rows, :], s_ref[...],
                                   tables)
        y_ref[rows, :] = _ret_gate(o, g_ref[rows, :])
        return 0

    lax.fori_loop(0, n_chunks, chunk, 0, unroll=8)


def _retention_prompt(lg, z, *, n_seq, seq, heads, dk, dv):
    v_off = 2 * heads * dk // dv
    in_specs = [pl.BlockSpec(memory_space=pltpu.SMEM),
                pl.BlockSpec((seq, dk), lambda b, h: (b, h)),
                pl.BlockSpec((seq, dk), lambda b, h: (b, heads + h)),
                pl.BlockSpec((seq, dv), lambda b, h: (b, v_off + h)),
                pl.BlockSpec((seq, dv), lambda b, h: (b, v_off + heads + h))]
    return pl.pallas_call(
        functools.partial(_retention_prompt_kernel, n_chunks=seq // RET_CHUNK, k_scale=dk ** -0.5),
        grid=(n_seq, heads),
        in_specs=in_specs,
        out_specs=[pl.BlockSpec((seq, dv), lambda b, h: (b, h)),
                   pl.BlockSpec((None, None, dk, dv), lambda b, h: (b, h, 0, 0))],
        out_shape=[jax.ShapeDtypeStruct((z.shape[0], heads * dv), BF16),
                   jax.ShapeDtypeStruct((n_seq, heads, dk, dv), F32)],
        compiler_params=_params(("parallel", "parallel")),
        name="retention_prompt",
    )(lg, z, z, z, z)


def _retention_sample_kernel(lg_ref, z_ref, s0_ref, y_ref, s_ref, *, heads, dk, dv, chunk_len,
                             k_scale):
    rows = z_ref.shape[0]

    def padded(lo, width):
        a = z_ref[:, lo:lo + width]
        return jnp.concatenate([a, jnp.zeros((RET_CHUNK - rows, width), a.dtype)], axis=0)

    for h in range(heads):
        tables = _ret_tables(lg_ref[h], chunk_len, k_scale)
        v_lo = 2 * heads * dk + h * dv
        o, s_ref[h] = _ret_chunk(padded(h * dk, dk), padded((heads + h) * dk, dk),
                                 padded(v_lo, dv), s0_ref[h], tables)
        y_ref[:, h * dv:(h + 1) * dv] = _ret_gate(
            o[:rows], z_ref[:, v_lo + heads * dv:v_lo + (heads + 1) * dv])


def _retention_sample(lg, z, state, *, chunk_len):
    n_seq, rows, n = z.shape
    _, heads, dk, dv = state.shape
    per_b3 = lambda b: (b, 0, 0)
    per_b4 = lambda b: (b, 0, 0, 0)
    return pl.pallas_call(
        functools.partial(_retention_sample_kernel, heads=heads, dk=dk, dv=dv, chunk_len=chunk_len,
                          k_scale=dk ** -0.5),
        grid=(n_seq,),
        in_specs=[pl.BlockSpec(memory_space=pltpu.SMEM),
                  pl.BlockSpec((None, rows, n), per_b3),
                  pl.BlockSpec((None, heads, dk, dv), per_b4)],
        out_specs=[pl.BlockSpec((None, rows, heads * dv), per_b3),
                   pl.BlockSpec((None, heads, dk, dv), per_b4)],
        out_shape=[jax.ShapeDtypeStruct((n_seq, rows, heads * dv), BF16),
                   jax.ShapeDtypeStruct(state.shape, F32)],
        compiler_params=_params(("parallel",)),
        name="retention_sample",
    )(lg, z, state)


def _row_tile(m):
    return min(m, 512)


def kernel(x_prompt, x_sample, cache_k, cache_v, state_ret, page_table, norm_mix, norm_ffn,
           w_attn_in, q_norm_g, k_norm_g, lambda_q1, lambda_k1, lambda_q2, lambda_k2, subln_g,
           w_attn_out, w_ret_in, w_ret_out, w_up, w_down):
    batch, seq, d = x_prompt.shape
    dec_batch, dec_seq, _ = x_sample.shape
    depth = norm_mix.shape[0]
    half = q_norm_g.shape[1]
    heads_a = d // (2 * half)
    heads_r, dk, dv = state_ret.shape[2:]
    n_pool = cache_k.shape[1]
    assert 2 * half == LANES and cache_k.shape[2] == PAGE_SIZE and dec_seq & (dec_seq - 1) == 0

    xp = x_prompt.reshape(batch * seq, d)
    xs = x_sample.reshape(dec_batch * dec_seq, d)
    tp, ts = _row_tile(xp.shape[0]), _row_tile(xs.shape[0])
    ck = cache_k.reshape(cache_k.shape[0], n_pool, PAGE_SIZE * heads_a, LANES)
    cv = cache_v.reshape(cache_v.shape[0], n_pool, PAGE_SIZE * heads_a, LANES)
    slopes = 2.0 ** (-8.0 * jnp.arange(1, heads_a + 1, dtype=F32) / heads_a) * LOG2E
    log_gamma = jnp.log1p(-(2.0 ** (-5.0 - jnp.arange(heads_r, dtype=F32))))

    kp_l, vp_l, ks_l, vs_l, sp_l, ss_l = [], [], [], [], [], []
    for i in range(depth):
        g_mix = norm_mix[i].reshape(1, d)
        g_ffn = norm_ffn[i].reshape(1, d)
        wu, wd = w_up[i].astype(BF16), w_down[i].astype(BF16)
        if i % 2 == 0:
            a = i // 2
            lam_init = 0.8 - 0.6 * math.exp(-0.3 * i)
            w_in, wo = w_attn_in[a].astype(BF16), w_attn_out[a].astype(BF16)
            qg = jnp.tile(q_norm_g[a], d // half).reshape(1, d)
            kg = jnp.tile(k_norm_g[a], d // half).reshape(1, d)
            sg = subln_g[a].reshape(1, LANES)
            lam4 = jnp.stack([lambda_q1[a], lambda_k1[a], lambda_q2[a], lambda_k2[a]])
            qb, kf, kb, vf, vb = _qkv(xp, g_mix, w_in, qg, kg, tm=tp, half=half)
            qbs, kfs, kbs, vfs, vbs = _qkv(xs, g_mix, w_in, qg, kg, tm=ts, half=half)
            op = _attn_prompt(slopes, lam4, qb, kb, vb, sg, batch=batch, seq=seq, heads=heads_a,
                              half=half, lam_init=lam_init, tile=min(2048, seq))
            os_ = _attn_sample(page_table, slopes, lam4, qbs, kbs, vbs, sg, ck, cv, layer=a,
                               heads=heads_a, dec_seq=dec_seq, half=half, lam_init=lam_init)
            kp_l.append(kf.reshape(batch, seq, heads_a, LANES))
            vp_l.append(vf.reshape(batch, seq, heads_a, LANES))
            ks_l.append(kfs.reshape(dec_batch, dec_seq, heads_a, LANES))
            vs_l.append(vfs.reshape(dec_batch, dec_seq, heads_a, LANES))
        else:
            r = i // 2
            w_in, wo = w_ret_in[r].astype(BF16), w_ret_out[r].astype(BF16)
            zp = _ret_in(xp, g_mix, w_in, tm=tp)
            zs = _ret_in(xs, g_mix, w_in, tm=ts)
            op, sp = _retention_prompt(log_gamma, zp, n_seq=batch, seq=seq, heads=heads_r,
                                       dk=dk, dv=dv)
            zs = jnp.pad(zs.reshape(dec_batch, dec_seq, -1),
                         ((0, 0), (0, SAMPLE_ROWS - dec_seq), (0, 0)))
            os_, ss = _retention_sample(log_gamma, zs, state_ret[r], chunk_len=dec_seq)
            os_ = os_[:, :dec_seq].reshape(dec_batch * dec_seq, -1)
            sp_l.append(sp)
            ss_l.append(ss)
        xp = _proj_mlp(xp, op, wo, g_ffn, wu, wd, tm=tp)
        xs = _proj_mlp(xs, os_, wo, g_ffn, wu, wd, tm=ts)

    return (xp.reshape(batch, seq, d), xs.reshape(dec_batch, dec_seq, d),
            jnp.stack(kp_l), jnp.stack(vp_l), jnp.stack(ks_l), jnp.stack(vs_l),
            jnp.stack(sp_l), jnp.stack(ss_l))
```

```python
import functools
import math

import jax
import jax.numpy as jnp
from jax import lax
from jax.experimental import pallas as pl
from jax.experimental.pallas import tpu as pltpu

F32 = jnp.float32
BF16 = jnp.bfloat16
EPS = 1e-6
NEG = -1e30
LANES = 128
VMEM_LIMIT = 56 * 1024 * 1024
PAGES_PER_STEP = 8
PAGE_SLOTS = 3
RET_CHUNK = 128
RET_CHUNK_PROMPT = 256
RET_UNROLL = 8
PAGE_SIZE = 128
ATTN_KEY_STEP = 256
SAMPLE_SEQS_PER_STEP = 2
SAMPLE_ROWS = 16
LOG2E = math.log2(math.e)


def _params(sem):
    return pltpu.CompilerParams(dimension_semantics=sem, vmem_limit_bytes=VMEM_LIMIT)


def _resident(shape):
    nd = len(shape)
    return pl.BlockSpec(shape, lambda *_: (0,) * nd, pipeline_mode=pl.Buffered(1))


def _rmsnorm_bf16(x, g):
    return (x * lax.rsqrt(jnp.mean(x * x, axis=-1, keepdims=True) + EPS) * g).astype(BF16)


def _dot(a, b):
    return jnp.dot(a, b, preferred_element_type=F32)


def _dot_nt(a, b):
    return lax.dot_general(a, b, (((1,), (1,)), ((), ())), preferred_element_type=F32)


def _qkv_kernel(x_ref, g_ref, w_ref, qg_ref, kg_ref, qb_ref, kf_ref, kb_ref, vf_ref, vb_ref,
                *, d, half, q_scale):
    h = _rmsnorm_bf16(x_ref[...], g_ref[...])
    tm = h.shape[0]
    n_heads = d // LANES
    lo = lax.broadcasted_iota(jnp.int32, (tm, LANES), 1) < half
    inv = 1.0 / half

    def half_norm(a, gain):
        sq = a * a
        s_lo = jnp.sum(jnp.where(lo, sq, 0.0), axis=-1, keepdims=True)
        s_hi = jnp.sum(jnp.where(lo, 0.0, sq), axis=-1, keepdims=True)
        r = jnp.where(lo, lax.rsqrt(s_lo * inv + EPS), lax.rsqrt(s_hi * inv + EPS))
        return a * r * gain

    q = _dot(h, w_ref[:, 0:d])
    for c in range(d // LANES):
        sl = slice(c * LANES, (c + 1) * LANES)
        qb_ref[:, sl] = (half_norm(q[:, sl], qg_ref[:, sl]) * q_scale).astype(BF16)
    k = _dot(h, w_ref[:, d:2 * d])
    for c in range(d // LANES):
        sl = slice(c * LANES, (c + 1) * LANES)
        kn = half_norm(k[:, sl], kg_ref[:, sl])
        kf_ref[pl.ds(c, tm, stride=n_heads), :] = kn
        kb_ref[:, sl] = kn.astype(BF16)
    v = _dot(h, w_ref[:, 2 * d:3 * d])
    for c in range(n_heads):
        vf_ref[pl.ds(c, tm, stride=n_heads), :] = v[:, c * LANES:(c + 1) * LANES]
    vb_ref[...] = v.astype(BF16)


def _qkv(x, g, w, qg, kg, *, tm, half):
    m, d = x.shape
    row = lambda i: (i, 0)
    n_heads = d // LANES
    bf = jax.ShapeDtypeStruct((m, d), BF16)
    ff = jax.ShapeDtypeStruct((m * n_heads, LANES), F32)
    bf_spec = pl.BlockSpec((tm, d), row)
    ff_spec = pl.BlockSpec((tm * n_heads, LANES), row)
    outs = [bf, ff, bf, ff, bf]
    return pl.pallas_call(
        functools.partial(_qkv_kernel, d=d, half=half, q_scale=half ** -0.5 * LOG2E),
        grid=(m // tm,),
        in_specs=[pl.BlockSpec((tm, d), row), _resident((1, d)), _resident(w.shape),
                  _resident((1, d)), _resident((1, d))],
        out_specs=[bf_spec, ff_spec, bf_spec, ff_spec, bf_spec],
        out_shape=outs,
        compiler_params=_params(("parallel",)),
        name="qkv_proj",
    )(x, g, w, qg, kg)


def _lambda_11(l_ref, lam_init):
    a = jnp.sum(l_ref[0:1, :] * l_ref[1:2, :], axis=-1, keepdims=True)
    b = jnp.sum(l_ref[2:3, :] * l_ref[3:4, :], axis=-1, keepdims=True)
    return jnp.exp(a) - jnp.exp(b) + lam_init


def _subln_bf16(o, sg, lam_init):
    r = lax.rsqrt(jnp.mean(o * o, axis=-1, keepdims=True) + EPS)
    return (o * r * sg * (1.0 - lam_init)).astype(BF16)


def _softmax_step(s, v, m, l, acc):
    m_new = jnp.maximum(m, jnp.max(s, axis=-1, keepdims=True))
    alpha = jnp.exp2(m - m_new)
    p = jnp.exp2(s - m_new)
    l_new = alpha * l + jnp.sum(p, axis=-1, keepdims=True)
    acc_new = alpha * acc + _dot(p.astype(BF16), v)
    return m_new, l_new, acc_new


def _attn_prompt_kernel(slope_ref, lam_ref, q_ref, k_ref, v_ref, sg_ref, mask_ref, o_ref,
                        qt_scr, vt_scr, *, seq, tile, kstep, half, lam_init):
    slope = slope_ref[pl.program_id(1)]
    lam = _lambda_11(lam_ref, lam_init)
    extra = vt_scr.shape[0] - LANES
    qt_scr[...] = q_ref[...].astype(F32).T.astype(BF16)
    vt_scr[0:LANES, :] = v_ref[...].astype(F32).T.astype(BF16)
    vt_scr[LANES:, :] = jnp.ones((extra, seq), BF16)
    lo = lax.broadcasted_iota(jnp.int32, (LANES, tile), 0) < half
    zero = jnp.zeros((), BF16)
    lane = lax.broadcasted_iota(jnp.int32, (kstep, LANES), 1)
    b0 = slope * lax.broadcasted_iota(jnp.int32, (kstep, LANES), 0).astype(F32)
    b_hi = b0.astype(BF16).astype(F32)
    b_mid = (b0 - b_hi).astype(BF16).astype(F32)
    b_lo = b0 - b_hi - b_mid
    k_extra = jnp.where(lane == 0, b_hi, jnp.where(lane == 1, b_mid,
                                                   jnp.where(lane == 2, b_lo, 0.0))).astype(BF16)
    q_extra = jnp.where(lax.broadcasted_iota(jnp.int32, (LANES, 2 * tile), 0) < 3,
                        1.0, 0.0).astype(BF16)

    for qi in range(seq // tile):
        q0 = qi * tile
        qt = qt_scr[:, q0:q0 + tile]
        c1, c2 = jnp.where(lo, qt, zero), jnp.where(lo, zero, qt)
        q2t = jnp.concatenate(
            [c[:, j:j + kstep] for j in range(0, tile, kstep) for c in (c1, c2)], axis=1)
        q2t = jnp.concatenate([q2t, q_extra], axis=0)
        m = jnp.full((1, 2 * tile), NEG, F32)
        acc = jnp.zeros((LANES + extra, 2 * tile), F32)
        for k0 in range(0, q0 + tile, kstep):
            first = 2 * max(k0 - q0, 0)
            s = _dot(jnp.concatenate([k_ref[k0:k0 + kstep, :], k_extra], axis=1), q2t[:, first:])
            if k0 >= q0:
                blk = s[:, :2 * kstep] + mask_ref[...]
                s = blk if first + 2 * kstep == 2 * tile else jnp.concatenate(
                    [blk, s[:, 2 * kstep:]], axis=1)
            c = slope * float(k0 - q0)
            m_old = m[:, first:]
            m_new = jnp.maximum(m_old, jnp.max(s, axis=0, keepdims=True) + c)
            alpha = jnp.exp2(m_old - m_new)
            p = jnp.exp2(s - (m_new - c))
            acc_new = alpha * acc[:, first:] + _dot(vt_scr[:, k0:k0 + kstep], p.astype(BF16))
            if first:
                m_new = jnp.concatenate([m[:, :first], m_new], axis=1)
                acc_new = jnp.concatenate([acc[:, :first], acc_new], axis=1)
            m, acc = m_new, acc_new
        t = acc[:LANES] / acc[LANES:LANES + 1]
        o = [t[:, j:j + kstep] - lam * t[:, j + kstep:j + 2 * kstep]
             for j in range(0, 2 * tile, 2 * kstep)]
        o = o[0] if len(o) == 1 else jnp.concatenate(o, axis=1)
        r = lax.rsqrt(jnp.mean(o * o, axis=0, keepdims=True) + EPS)
        o_ref[q0:q0 + tile, :] = ((o * r).T * sg_ref[...] * (1.0 - lam_init)).astype(BF16)


def _attn_prompt(slopes, lam4, qb, kb, vb, sg, *, batch, seq, heads, half, lam_init, tile):
    m, d = qb.shape
    blk = pl.BlockSpec((seq, LANES), lambda b, h: (b, h))
    kstep = min(ATTN_KEY_STEP, tile)
    key = jnp.arange(kstep)[:, None]
    qry = jnp.arange(2 * kstep)[None, :] % kstep
    mask = jnp.where(key <= qry, 0.0, NEG).astype(F32)
    return pl.pallas_call(
        functools.partial(_attn_prompt_kernel, seq=seq, tile=tile, kstep=kstep,
                          half=half, lam_init=lam_init),
        grid=(batch, heads),
        in_specs=[pl.BlockSpec(memory_space=pltpu.SMEM), _resident(lam4.shape),
                  blk, blk, blk, _resident(sg.shape), _resident(mask.shape)],
        out_specs=blk,
        out_shape=jax.ShapeDtypeStruct((m, d), BF16),
        scratch_shapes=[pltpu.VMEM((LANES, seq), BF16), pltpu.VMEM((LANES + 16, seq), BF16)],
        compiler_params=_params(("parallel", "parallel")),
        name="attn_prompt",
    )(slopes, lam4, qb, kb, vb, sg, mask)


def _page_part(s):
    m = jnp.max(s, axis=-1, keepdims=True)
    p = jnp.exp2(s - m)
    return m, jnp.sum(p, axis=-1, keepdims=True), p


def _attn_sample_kernel(pt_ref, lam_ref, q2_ref, kn_ref, vn_ref, sg_ref, mb_ref, mbn_ref, sc_ref,
                        ck_ref, cv_ref, o_ref, k_buf, v_buf, sem, m_scr, l_scr, acc_scr,
                        *, layer, pages, past, lam_init):
    j = pl.program_id(1)
    n_j = pl.num_programs(1)
    step = pl.program_id(0) * n_j + j
    n_steps = pl.num_programs(0) * n_j
    nr = mb_ref.shape[0]

    def page_copies(s):
        slot = s % PAGE_SLOTS
        sb, sj = s // n_j, s % n_j
        out = []
        for i in range(pages):
            page = pt_ref[sb, sj * pages + i]
            out.append(pltpu.make_async_copy(ck_ref.at[layer, page], k_buf.at[slot, i],
                                             sem.at[slot, i]))
            out.append(pltpu.make_async_copy(cv_ref.at[layer, page], v_buf.at[slot, i],
                                             sem.at[slot, pages + i]))
        return out

    @pl.when(step == 0)
    def _():
        for s in range(PAGE_SLOTS - 1):
            for c in page_copies(s):
                c.start()

    @pl.when(step + PAGE_SLOTS - 1 < n_steps)
    def _():
        for c in page_copies(step + PAGE_SLOTS - 1):
            c.start()

    @pl.when(j == 0)
    def _():
        m_scr[...] = jnp.full(m_scr.shape, NEG, F32)
        l_scr[...] = jnp.zeros(l_scr.shape, F32)
        acc_scr[...] = jnp.zeros(acc_scr.shape, F32)

    def merge(parts):
        m_old = m_scr[...]
        m_new = m_old
        for m, _, _ in parts:
            m_new = jnp.maximum(m_new, m)
        a = jnp.exp2(m_old - m_new)
        l = a * l_scr[...]
        acc = a * acc_scr[...]
        for m, lp, op in parts:
            w = jnp.exp2(m - m_new)
            l = l + w * lp
            acc = acc + w * op
        m_scr[...] = m_new
        l_scr[...] = l
        acc_scr[...] = acc

    for c in page_copies(step):
        c.wait()
    slot = step % PAGE_SLOTS
    q2 = q2_ref[...]
    mb = mb_ref[...]
    sc = sc_ref[...]
    parts = []
    for pr in range(pages // 2):
        ia, ib = 2 * pr, 2 * pr + 1
        kcat = jnp.concatenate([k_buf[slot, ia].astype(BF16), k_buf[slot, ib].astype(BF16)], axis=1)
        s2 = _dot_nt(q2, kcat)
        ma, la, pa = _page_part(s2[:nr] + mb)
        mb_, lb, pb = _page_part(s2[nr:] + mb)
        vcat = jnp.concatenate([v_buf[slot, ia].astype(BF16), v_buf[slot, ib].astype(BF16)], axis=1)
        o2 = _dot(jnp.concatenate([pa, pb], axis=0).astype(BF16), vcat)
        off = ((j * pages + ia) * PAGE_SIZE - past).astype(F32)
        parts.append((ma + sc * off, la, o2[:nr, :LANES]))
        parts.append((mb_ + sc * (off + PAGE_SIZE), lb, o2[nr:, LANES:]))
    merge(parts)

    @pl.when(j == n_j - 1)
    def _():
        mn, ln, pn = _page_part(_dot_nt(q2[:nr, :LANES], kn_ref[...]) + mbn_ref[...])
        merge([(mn, ln, _dot(pn.astype(BF16), vn_ref[...]))])
        t = acc_scr[...] / l_scr[...]
        o = t[:nr // 2] - _lambda_11(lam_ref, lam_init) * t[nr // 2:]
        o_ref[...] = _subln_bf16(o, sg_ref[...], lam_init)


def _attn_sample(page_table, slopes, lam4, qbs, kbs, vbs, sg, ck, cv, *, layer, heads, dec_seq,
                 half, lam_init):
    nb = page_table.shape[0]
    d = qbs.shape[1]
    n_pages = page_table.shape[1]
    pages = PAGES_PER_STEP
    past = n_pages * PAGE_SIZE
    hq = heads * dec_seq
    nr = 2 * hq
    assert nb * (n_pages // pages) >= PAGE_SLOTS - 1 and n_pages % pages == 0
    q4 = qbs.reshape(nb, dec_seq, heads, LANES).transpose(0, 2, 1, 3).reshape(nb, hq, LANES)
    lo = jnp.arange(LANES) < half
    zq = jnp.zeros_like(q4)
    qa = jnp.concatenate([jnp.where(lo, q4, zq), jnp.where(lo, zq, q4)], axis=1)
    za = jnp.zeros_like(qa)
    q2 = jnp.concatenate([jnp.concatenate([qa, za], axis=2),
                          jnp.concatenate([za, qa], axis=2)], axis=1)
    n_new = LANES
    pad = ((0, 0), (0, n_new - hq), (0, 0))
    kn = jnp.pad(kbs.reshape(nb, hq, LANES), pad)
    vn = jnp.pad(vbs.reshape(nb, hq, LANES), pad)
    r = jnp.arange(nr)
    r_head, r_query = (r % hq) // dec_seq, r % dec_seq
    r_slope = slopes[r_head][:, None]

    def table(n_cols, causal):
        c = jnp.arange(n_cols)
        c_head, c_tok = c % heads, c // heads
        ok = c_head[None, :] == r_head[:, None]
        if causal:
            ok = ok & (c_tok[None, :] <= r_query[:, None])
        return jnp.where(ok, r_slope * c_tok[None, :].astype(F32), NEG)

    mb = table(PAGE_SIZE * heads, False)
    mbn = table(n_new, True)

    per_b = lambda b, j, pt: (b, 0, 0)
    const2 = lambda b, j, pt: (0, 0)
    page_rows = PAGE_SIZE * heads
    grid_spec = pltpu.PrefetchScalarGridSpec(
        num_scalar_prefetch=1,
        grid=(nb, n_pages // pages),
        in_specs=[pl.BlockSpec(lam4.shape, const2),
                  pl.BlockSpec((None, 2 * nr, 2 * LANES), per_b),
                  pl.BlockSpec((None, n_new, LANES), per_b),
                  pl.BlockSpec((None, n_new, LANES), per_b),
                  pl.BlockSpec(sg.shape, const2),
                  pl.BlockSpec(mb.shape, const2),
                  pl.BlockSpec(mbn.shape, const2),
                  pl.BlockSpec(r_slope.shape, const2),
                  pl.BlockSpec(memory_space=pl.ANY),
                  pl.BlockSpec(memory_space=pl.ANY)],
        out_specs=pl.BlockSpec((None, hq, LANES), per_b),
        scratch_shapes=[pltpu.VMEM((PAGE_SLOTS, pages, page_rows, LANES), F32),
                        pltpu.VMEM((PAGE_SLOTS, pages, page_rows, LANES), F32),
                        pltpu.SemaphoreType.DMA((PAGE_SLOTS, 2 * pages)),
                        pltpu.VMEM((nr, 1), F32), pltpu.VMEM((nr, 1), F32),
                        pltpu.VMEM((nr, LANES), F32)],
    )
    out = pl.pallas_call(
        functools.partial(_attn_sample_kernel, layer=layer, pages=pages, past=past,
                          lam_init=lam_init),
        grid_spec=grid_spec,
        out_shape=jax.ShapeDtypeStruct((nb, hq, LANES), BF16),
        compiler_params=_params(("arbitrary", "arbitrary")),
        name="attn_sample",
    )(page_table, lam4, q2, kn, vn, sg, mb, mbn, r_slope, ck, cv)
    return out.reshape(nb, heads, dec_seq, LANES).transpose(0, 2, 1, 3).reshape(nb * dec_seq, d)


def _proj_mlp_kernel(x_ref, a_ref, wo_ref, g_ref, wu_ref, wd_ref, o_ref, *, ff_chunk):
    x1 = x_ref[...] + _dot(a_ref[...], wo_ref[...])
    h = _rmsnorm_bf16(x1, g_ref[...])
    acc = x1
    for c in range(wu_ref.shape[1] // ff_chunk):
        sl = slice(c * ff_chunk, (c + 1) * ff_chunk)
        u = jnp.maximum(_dot(h, wu_ref[:, sl]), 0.0)
        acc = acc + _dot((u * u).astype(BF16), wd_ref[sl, :])
    o_ref[...] = acc


def _proj_mlp(x, a, wo, g, wu, wd, *, tm):
    m, d = x.shape
    row = lambda i: (i, 0)
    return pl.pallas_call(
        functools.partial(_proj_mlp_kernel, ff_chunk=min(1024, wu.shape[1])),
        grid=(m // tm,),
        in_specs=[pl.BlockSpec((tm, d), row), pl.BlockSpec((tm, a.shape[1]), row),
                  _resident(wo.shape), _resident((1, d)), _resident(wu.shape), _resident(wd.shape)],
        out_specs=pl.BlockSpec((tm, d), row),
        out_shape=jax.ShapeDtypeStruct((m, d), F32),
        compiler_params=_params(("parallel",)),
        name="proj_mlp",
    )(x, a, wo, g, wu, wd)


def _ret_in_kernel(x_ref, g_ref, w_ref, z_ref, *, n_chunk):
    h = _rmsnorm_bf16(x_ref[...], g_ref[...])
    for c in range(w_ref.shape[1] // n_chunk):
        sl = slice(c * n_chunk, (c + 1) * n_chunk)
        z_ref[:, sl] = _dot(h, w_ref[:, sl]).astype(BF16)


def _ret_in(x, g, w, *, tm):
    m, d = x.shape
    n = w.shape[1]
    row = lambda i: (i, 0)
    return pl.pallas_call(
        functools.partial(_ret_in_kernel, n_chunk=min(1024, n)),
        grid=(m // tm,),
        in_specs=[pl.BlockSpec((tm, d), row), _resident((1, d)), _resident(w.shape)],
        out_specs=pl.BlockSpec((tm, n), row),
        out_shape=jax.ShapeDtypeStruct((m, n), BF16),
        compiler_params=_params(("parallel",)),
        name="ret_in_proj",
    )(x, g, w)


def _ret_tables(lg, c, chunk_len, k_scale):
    idx = lax.broadcasted_iota(jnp.int32, (c, 1), 0).astype(F32)
    diff = (lax.broadcasted_iota(jnp.int32, (c, c), 0)
            - lax.broadcasted_iota(jnp.int32, (c, c), 1)).astype(F32)
    dmat = jnp.where(diff >= 0, jnp.exp(lg * jnp.maximum(diff, 0.0)), 0.0) * k_scale
    q_dec = jnp.exp(lg * (idx + 1.0))
    k_dec = jnp.exp(lg * (chunk_len - 1.0 - idx)) * k_scale
    s_dec = jnp.exp(jnp.full((1, 1), lg * chunk_len, F32))
    return dmat, q_dec, k_dec, s_dec


def _ret_chunk(q, k, v, state, tables):
    dmat, q_dec, k_dec, s_dec = tables
    qk = _dot_nt(q, k) * dmat
    o = _dot(qk.astype(BF16), v) + q_dec * _dot(q, state.astype(BF16))
    kd_t = (k.astype(F32) * k_dec).T.astype(BF16)
    return o, s_dec * state + _dot(kd_t, v)


def _ret_gate(o, g):
    y = o * lax.rsqrt(jnp.mean(o * o, axis=-1, keepdims=True) + EPS)
    g = g.astype(F32)
    return (g * jax.nn.sigmoid(g) * y).astype(BF16)


def _retention_prompt_kernel(lg_ref, q_ref, k_ref, v_ref, g_ref, y_ref, s_ref, *, c, n_chunks,
                             k_scale):
    tables = _ret_tables(lg_ref[pl.program_id(1)], c, c, k_scale)
    s_ref[...] = jnp.zeros(s_ref.shape, F32)

    def chunk(i, _):
        rows = pl.ds(pl.multiple_of(i * c, c), c)
        o, s_ref[...] = _ret_chunk(q_ref[rows, :], k_ref[rows, :], v_ref[rows, :], s_ref[...],
                                   tables)
        y_ref[rows, :] = _ret_gate(o, g_ref[rows, :])
        return 0

    lax.fori_loop(0, n_chunks, chunk, 0, unroll=min(n_chunks, RET_UNROLL))


def _retention_prompt(lg, z, *, n_seq, seq, heads, dk, dv):
    v_off = 2 * heads * dk // dv
    in_specs = [pl.BlockSpec(memory_space=pltpu.SMEM),
                pl.BlockSpec((seq, dk), lambda b, h: (b, h)),
                pl.BlockSpec((seq, dk), lambda b, h: (b, heads + h)),
                pl.BlockSpec((seq, dv), lambda b, h: (b, v_off + h)),
                pl.BlockSpec((seq, dv), lambda b, h: (b, v_off + heads + h))]
    return pl.pallas_call(
        functools.partial(_retention_prompt_kernel, c=min(RET_CHUNK_PROMPT, seq),
                          n_chunks=seq // min(RET_CHUNK_PROMPT, seq), k_scale=dk ** -0.5),
        grid=(n_seq, heads),
        in_specs=in_specs,
        out_specs=[pl.BlockSpec((seq, dv), lambda b, h: (b, h)),
                   pl.BlockSpec((None, None, dk, dv), lambda b, h: (b, h, 0, 0))],
        out_shape=[jax.ShapeDtypeStruct((z.shape[0], heads * dv), BF16),
                   jax.ShapeDtypeStruct((n_seq, heads, dk, dv), F32)],
        compiler_params=_params(("parallel", "parallel")),
        name="retention_prompt",
    )(lg, z, z, z, z)


def _retention_sample_kernel(lg_ref, z_ref, s0_ref, y_ref, s_ref, *, heads, dk, dv, chunk_len,
                             k_scale):
    n_here, rows = z_ref.shape[:2]

    def padded(i, lo, width):
        a = z_ref[i, :, lo:lo + width]
        return jnp.concatenate([a, jnp.zeros((RET_CHUNK - rows, width), a.dtype)], axis=0)

    for h in range(heads):
        tables = _ret_tables(lg_ref[h], RET_CHUNK, chunk_len, k_scale)
        v_lo = 2 * heads * dk + h * dv
        for i in range(n_here):
            o, s_ref[i, h] = _ret_chunk(padded(i, h * dk, dk), padded(i, (heads + h) * dk, dk),
                                        padded(i, v_lo, dv), s0_ref[i, h], tables)
            y_ref[i, :, h * dv:(h + 1) * dv] = _ret_gate(
                o[:rows], z_ref[i, :, v_lo + heads * dv:v_lo + (heads + 1) * dv])


def _retention_sample(lg, z, state, *, chunk_len):
    n_seq, rows, n = z.shape
    _, heads, dk, dv = state.shape
    per = SAMPLE_SEQS_PER_STEP if n_seq % SAMPLE_SEQS_PER_STEP == 0 else 1
    per_b3 = lambda b: (b, 0, 0)
    per_b4 = lambda b: (b, 0, 0, 0)
    return pl.pallas_call(
        functools.partial(_retention_sample_kernel, heads=heads, dk=dk, dv=dv, chunk_len=chunk_len,
                          k_scale=dk ** -0.5),
        grid=(n_seq // per,),
        in_specs=[pl.BlockSpec(memory_space=pltpu.SMEM),
                  pl.BlockSpec((per, rows, n), per_b3),
                  pl.BlockSpec((per, heads, dk, dv), per_b4)],
        out_specs=[pl.BlockSpec((per, rows, heads * dv), per_b3),
                   pl.BlockSpec((per, heads, dk, dv), per_b4)],
        out_shape=[jax.ShapeDtypeStruct((n_seq, rows, heads * dv), BF16),
                   jax.ShapeDtypeStruct(state.shape, F32)],
        compiler_params=_params(("parallel",)),
        name="retention_sample",
    )(lg, z, state)


def _row_tile(m):
    return min(m, 512)


def kernel(x_prompt, x_sample, cache_k, cache_v, state_ret, page_table, norm_mix, norm_ffn,
           w_attn_in, q_norm_g, k_norm_g, lambda_q1, lambda_k1, lambda_q2, lambda_k2, subln_g,
           w_attn_out, w_ret_in, w_ret_out, w_up, w_down):
    batch, seq, d = x_prompt.shape
    dec_batch, dec_seq, _ = x_sample.shape
    depth = norm_mix.shape[0]
    half = q_norm_g.shape[1]
    heads_a = d // (2 * half)
    heads_r, dk, dv = state_ret.shape[2:]
    n_pool = cache_k.shape[1]
    assert 2 * half == LANES and cache_k.shape[2] == PAGE_SIZE and dec_seq & (dec_seq - 1) == 0

    xp = x_prompt.reshape(batch * seq, d)
    xs = x_sample.reshape(dec_batch * dec_seq, d)
    tp, ts = _row_tile(xp.shape[0]), _row_tile(xs.shape[0])
    ck = cache_k.reshape(cache_k.shape[0], n_pool, PAGE_SIZE * heads_a, LANES)
    cv = cache_v.reshape(cache_v.shape[0], n_pool, PAGE_SIZE * heads_a, LANES)
    slopes = 2.0 ** (-8.0 * jnp.arange(1, heads_a + 1, dtype=F32) / heads_a) * LOG2E
    log_gamma = jnp.log1p(-(2.0 ** (-5.0 - jnp.arange(heads_r, dtype=F32))))

    kp_l, vp_l, ks_l, vs_l, sp_l, ss_l = [], [], [], [], [], []
    for i in range(depth):
        g_mix = norm_mix[i].reshape(1, d)
        g_ffn = norm_ffn[i].reshape(1, d)
        wu, wd = w_up[i].astype(BF16), w_down[i].astype(BF16)
        if i % 2 == 0:
            a = i // 2
            lam_init = 0.8 - 0.6 * math.exp(-0.3 * i)
            w_in, wo = w_attn_in[a].astype(BF16), w_attn_out[a].astype(BF16)
            qg = jnp.tile(q_norm_g[a], d // half).reshape(1, d)
            kg = jnp.tile(k_norm_g[a], d // half).reshape(1, d)
            sg = subln_g[a].reshape(1, LANES)
            lam4 = jnp.stack([lambda_q1[a], lambda_k1[a], lambda_q2[a], lambda_k2[a]])
            qb, kf, kb, vf, vb = _qkv(xp, g_mix, w_in, qg, kg, tm=tp, half=half)
            qbs, kfs, kbs, vfs, vbs = _qkv(xs, g_mix, w_in, qg, kg, tm=ts, half=half)
            op = _attn_prompt(slopes, lam4, qb, kb, vb, sg, batch=batch, seq=seq, heads=heads_a,
                              half=half, lam_init=lam_init, tile=min(2048, seq))
            os_ = _attn_sample(page_table, slopes, lam4, qbs, kbs, vbs, sg, ck, cv, layer=a,
                               heads=heads_a, dec_seq=dec_seq, half=half, lam_init=lam_init)
            kp_l.append(kf.reshape(batch, seq, heads_a, LANES))
            vp_l.append(vf.reshape(batch, seq, heads_a, LANES))
            ks_l.append(kfs.reshape(dec_batch, dec_seq, heads_a, LANES))
            vs_l.append(vfs.reshape(dec_batch, dec_seq, heads_a, LANES))
        else:
            r = i // 2
            w_in, wo = w_ret_in[r].astype(BF16), w_ret_out[r].astype(BF16)
            zp = _ret_in(xp, g_mix, w_in, tm=tp)
            zs = _ret_in(xs, g_mix, w_in, tm=ts)
            op, sp = _retention_prompt(log_gamma, zp, n_seq=batch, seq=seq, heads=heads_r,
                                       dk=dk, dv=dv)
            zs = jnp.pad(zs.reshape(dec_batch, dec_seq, -1),
                         ((0, 0), (0, SAMPLE_ROWS - dec_seq), (0, 0)))
            os_, ss = _retention_sample(log_gamma, zs, state_ret[r], chunk_len=dec_seq)
            os_ = os_[:, :dec_seq].reshape(dec_batch * dec_seq, -1)
            sp_l.append(sp)
            ss_l.append(ss)
        xp = _proj_mlp(xp, op, wo, g_ffn, wu, wd, tm=tp)
        xs = _proj_mlp(xs, os_, wo, g_ffn, wu, wd, tm=ts)

    return (xp.reshape(batch, seq, d), xs.reshape(dec_batch, dec_seq, d),
            jnp.stack(kp_l), jnp.stack(vp_l), jnp.stack(ks_l), jnp.stack(vs_l),
            jnp.stack(sp_l), jnp.stack(ss_l))
```

```python
import functools
import math

import jax
import jax.numpy as jnp
from jax import lax
from jax.experimental import pallas as pl
from jax.experimental.pallas import tpu as pltpu

F32 = jnp.float32
BF16 = jnp.bfloat16
EPS = 1e-6
NEG = -1e30
LANES = 128
VMEM_LIMIT = 56 * 1024 * 1024
PAGES_PER_STEP = 8
PAGE_SLOTS = 4
RET_CHUNK = 128
RET_CHUNK_PROMPT = 256
RET_UNROLL = 8
PAGE_SIZE = 128
ATTN_KEY_STEP = 256
SAMPLE_SEQS_PER_STEP = 4
SAMPLE_ROWS = 16
LOG2E = math.log2(math.e)


def _params(sem):
    return pltpu.CompilerParams(dimension_semantics=sem, vmem_limit_bytes=VMEM_LIMIT)


def _resident(shape):
    nd = len(shape)
    return pl.BlockSpec(shape, lambda *_: (0,) * nd, pipeline_mode=pl.Buffered(1))


def _rmsnorm_bf16(x, g):
    return (x * lax.rsqrt(jnp.mean(x * x, axis=-1, keepdims=True) + EPS) * g).astype(BF16)


def _dot(a, b):
    return jnp.dot(a, b, preferred_element_type=F32)


def _dot_nt(a, b):
    return lax.dot_general(a, b, (((1,), (1,)), ((), ())), preferred_element_type=F32)


def _qkv_kernel(x_ref, g_ref, w_ref, qg_ref, kg_ref, qb_ref, kf_ref, kb_ref, vf_ref, vb_ref,
                *, d, half, q_scale):
    h = _rmsnorm_bf16(x_ref[...], g_ref[...])
    tm = h.shape[0]
    n_heads = d // LANES
    lo = lax.broadcasted_iota(jnp.int32, (tm, LANES), 1) < half
    inv = 1.0 / half

    def half_norm(a, gain):
        sq = a * a
        s_lo = jnp.sum(jnp.where(lo, sq, 0.0), axis=-1, keepdims=True)
        s_hi = jnp.sum(jnp.where(lo, 0.0, sq), axis=-1, keepdims=True)
        r = jnp.where(lo, lax.rsqrt(s_lo * inv + EPS), lax.rsqrt(s_hi * inv + EPS))
        return a * r * gain

    q = _dot(h, w_ref[:, 0:d])
    for c in range(d // LANES):
        sl = slice(c * LANES, (c + 1) * LANES)
        qb_ref[:, sl] = (half_norm(q[:, sl], qg_ref[:, sl]) * q_scale).astype(BF16)
    k = _dot(h, w_ref[:, d:2 * d])
    for c in range(d // LANES):
        sl = slice(c * LANES, (c + 1) * LANES)
        kn = half_norm(k[:, sl], kg_ref[:, sl])
        kf_ref[pl.ds(c, tm, stride=n_heads), :] = kn
        kb_ref[:, sl] = kn.astype(BF16)
    v = _dot(h, w_ref[:, 2 * d:3 * d])
    for c in range(n_heads):
        vf_ref[pl.ds(c, tm, stride=n_heads), :] = v[:, c * LANES:(c + 1) * LANES]
    vb_ref[...] = v.astype(BF16)


def _qkv(x, g, w, qg, kg, *, tm, half):
    m, d = x.shape
    row = lambda i: (i, 0)
    n_heads = d // LANES
    bf = jax.ShapeDtypeStruct((m, d), BF16)
    ff = jax.ShapeDtypeStruct((m * n_heads, LANES), F32)
    bf_spec = pl.BlockSpec((tm, d), row)
    ff_spec = pl.BlockSpec((tm * n_heads, LANES), row)
    outs = [bf, ff, bf, ff, bf]
    return pl.pallas_call(
        functools.partial(_qkv_kernel, d=d, half=half, q_scale=half ** -0.5 * LOG2E),
        grid=(m // tm,),
        in_specs=[pl.BlockSpec((tm, d), row), _resident((1, d)), _resident(w.shape),
                  _resident((1, d)), _resident((1, d))],
        out_specs=[bf_spec, ff_spec, bf_spec, ff_spec, bf_spec],
        out_shape=outs,
        compiler_params=_params(("parallel",)),
        name="qkv_proj",
    )(x, g, w, qg, kg)


def _lambda_11(l_ref, lam_init):
    a = jnp.sum(l_ref[0:1, :] * l_ref[1:2, :], axis=-1, keepdims=True)
    b = jnp.sum(l_ref[2:3, :] * l_ref[3:4, :], axis=-1, keepdims=True)
    return jnp.exp(a) - jnp.exp(b) + lam_init


def _subln_bf16(o, sg, lam_init):
    r = lax.rsqrt(jnp.mean(o * o, axis=-1, keepdims=True) + EPS)
    return (o * r * sg * (1.0 - lam_init)).astype(BF16)


def _attn_prompt_kernel(slope_ref, lam_ref, q_ref, k_ref, v_ref, sg_ref, mask_ref, o_ref,
                        qt_scr, vt_scr, *, seq, tile, kstep, half, lam_init):
    slope = slope_ref[pl.program_id(1)]
    lam = _lambda_11(lam_ref, lam_init)
    extra = vt_scr.shape[0] - LANES
    qt_scr[...] = q_ref[...].astype(F32).T.astype(BF16)
    vt_scr[0:LANES, :] = v_ref[...].astype(F32).T.astype(BF16)
    vt_scr[LANES:, :] = jnp.ones((extra, seq), BF16)
    lo = lax.broadcasted_iota(jnp.int32, (LANES, tile), 0) < half
    zero = jnp.zeros((), BF16)
    lane = lax.broadcasted_iota(jnp.int32, (kstep, LANES), 1)
    b0 = slope * lax.broadcasted_iota(jnp.int32, (kstep, LANES), 0).astype(F32)
    b_hi = b0.astype(BF16).astype(F32)
    b_mid = (b0 - b_hi).astype(BF16).astype(F32)
    b_lo = b0 - b_hi - b_mid
    k_extra = jnp.where(lane == 0, b_hi, jnp.where(lane == 1, b_mid,
                                                   jnp.where(lane == 2, b_lo, 0.0))).astype(BF16)
    q_extra = jnp.where(lax.broadcasted_iota(jnp.int32, (LANES, 2 * tile), 0) < 3,
                        1.0, 0.0).astype(BF16)

    for qi in range(seq // tile):
        q0 = qi * tile
        qt = qt_scr[:, q0:q0 + tile]
        c1, c2 = jnp.where(lo, qt, zero), jnp.where(lo, zero, qt)
        q2t = jnp.concatenate(
            [c[:, j:j + kstep] for j in range(0, tile, kstep) for c in (c1, c2)], axis=1)
        q2t = jnp.concatenate([q2t, q_extra], axis=0)
        m = jnp.full((1, 2 * tile), NEG, F32)
        acc = jnp.zeros((LANES + extra, 2 * tile), F32)
        for k0 in range(0, q0 + tile, kstep):
            first = 2 * max(k0 - q0, 0)
            s = _dot(jnp.concatenate([k_ref[k0:k0 + kstep, :], k_extra], axis=1), q2t[:, first:])
            if k0 >= q0:
                blk = s[:, :2 * kstep] + mask_ref[...]
                s = blk if first + 2 * kstep == 2 * tile else jnp.concatenate(
                    [blk, s[:, 2 * kstep:]], axis=1)
            c = slope * float(k0 - q0)
            m_old = m[:, first:]
            m_new = jnp.maximum(m_old, jnp.max(s, axis=0, keepdims=True) + c)
            alpha = jnp.exp2(m_old - m_new)
            p = jnp.exp2(s - (m_new - c))
            acc_new = alpha * acc[:, first:] + _dot(vt_scr[:, k0:k0 + kstep], p.astype(BF16))
            if first:
                m_new = jnp.concatenate([m[:, :first], m_new], axis=1)
                acc_new = jnp.concatenate([acc[:, :first], acc_new], axis=1)
            m, acc = m_new, acc_new
        t = acc[:LANES] / acc[LANES:LANES + 1]
        o = [t[:, j:j + kstep] - lam * t[:, j + kstep:j + 2 * kstep]
             for j in range(0, 2 * tile, 2 * kstep)]
        o = o[0] if len(o) == 1 else jnp.concatenate(o, axis=1)
        r = lax.rsqrt(jnp.mean(o * o, axis=0, keepdims=True) + EPS)
        o_ref[q0:q0 + tile, :] = ((o * r).T * sg_ref[...] * (1.0 - lam_init)).astype(BF16)


def _attn_prompt(slopes, lam4, qb, kb, vb, sg, *, batch, seq, heads, half, lam_init, tile):
    m, d = qb.shape
    blk = pl.BlockSpec((seq, LANES), lambda b, h: (b, h))
    kstep = min(ATTN_KEY_STEP, tile)
    key = jnp.arange(kstep)[:, None]
    qry = jnp.arange(2 * kstep)[None, :] % kstep
    mask = jnp.where(key <= qry, 0.0, NEG).astype(F32)
    return pl.pallas_call(
        functools.partial(_attn_prompt_kernel, seq=seq, tile=tile, kstep=kstep,
                          half=half, lam_init=lam_init),
        grid=(batch, heads),
        in_specs=[pl.BlockSpec(memory_space=pltpu.SMEM), _resident(lam4.shape),
                  blk, blk, blk, _resident(sg.shape), _resident(mask.shape)],
        out_specs=blk,
        out_shape=jax.ShapeDtypeStruct((m, d), BF16),
        scratch_shapes=[pltpu.VMEM((LANES, seq), BF16), pltpu.VMEM((LANES + 16, seq), BF16)],
        compiler_params=_params(("parallel", "parallel")),
        name="attn_prompt",
    )(slopes, lam4, qb, kb, vb, sg, mask)


def _page_part(s):
    m = jnp.max(s, axis=-1, keepdims=True)
    p = jnp.exp2(s - m)
    return m, jnp.sum(p, axis=-1, keepdims=True), p


def _attn_sample_kernel(pt_ref, lam_ref, q2_ref, kn_ref, vn_ref, sg_ref, mb_ref, mbn_ref, sc_ref,
                        ck_ref, cv_ref, o_ref, k_buf, v_buf, sem, m_scr, l_scr, acc_scr,
                        *, layer, pages, past, lam_init):
    j = pl.program_id(1)
    n_j = pl.num_programs(1)
    step = pl.program_id(0) * n_j + j
    n_steps = pl.num_programs(0) * n_j
    nr = mb_ref.shape[0]

    def page_copies(s):
        slot = s % PAGE_SLOTS
        sb, sj = s // n_j, s % n_j
        out = []
        for i in range(pages):
            page = pt_ref[sb, sj * pages + i]
            out.append(pltpu.make_async_copy(ck_ref.at[layer, page], k_buf.at[slot, i],
                                             sem.at[slot, i]))
            out.append(pltpu.make_async_copy(cv_ref.at[layer, page], v_buf.at[slot, i],
                                             sem.at[slot, pages + i]))
        return out

    @pl.when(step == 0)
    def _():
        for s in range(PAGE_SLOTS - 1):
            for c in page_copies(s):
                c.start()

    @pl.when(step + PAGE_SLOTS - 1 < n_steps)
    def _():
        for c in page_copies(step + PAGE_SLOTS - 1):
            c.start()

    @pl.when(j == 0)
    def _():
        m_scr[...] = jnp.full(m_scr.shape, NEG, F32)
        l_scr[...] = jnp.zeros(l_scr.shape, F32)
        acc_scr[...] = jnp.zeros(acc_scr.shape, F32)

    def merge(parts):
        m_old = m_scr[...]
        m_new = m_old
        for m, _, _ in parts:
            m_new = jnp.maximum(m_new, m)
        a = jnp.exp2(m_old - m_new)
        l = a * l_scr[...]
        acc = a * acc_scr[...]
        for m, lp, op in parts:
            w = jnp.exp2(m - m_new)
            l = l + w * lp
            acc = acc + w * op
        m_scr[...] = m_new
        l_scr[...] = l
        acc_scr[...] = acc

    for c in page_copies(step):
        c.wait()
    slot = step % PAGE_SLOTS
    q2 = q2_ref[...]
    mb = mb_ref[...]
    sc = sc_ref[...]
    parts = []
    for pr in range(pages // 2):
        ia, ib = 2 * pr, 2 * pr + 1
        kcat = jnp.concatenate([k_buf[slot, ia].astype(BF16), k_buf[slot, ib].astype(BF16)], axis=1)
        s2 = _dot_nt(q2, kcat)
        ma, la, pa = _page_part(s2[:nr] + mb)
        mb_, lb, pb = _page_part(s2[nr:] + mb)
        vcat = jnp.concatenate([v_buf[slot, ia].astype(BF16), v_buf[slot, ib].astype(BF16)], axis=1)
        o2 = _dot(jnp.concatenate([pa, pb], axis=0).astype(BF16), vcat)
        off = ((j * pages + ia) * PAGE_SIZE - past).astype(F32)
        parts.append((ma + sc * off, la, o2[:nr, :LANES]))
        parts.append((mb_ + sc * (off + PAGE_SIZE), lb, o2[nr:, LANES:]))
    merge(parts)

    @pl.when(j == n_j - 1)
    def _():
        mn, ln, pn = _page_part(_dot_nt(q2[:nr, :LANES], kn_ref[...]) + mbn_ref[...])
        merge([(mn, ln, _dot(pn.astype(BF16), vn_ref[...]))])
        t = acc_scr[...] / l_scr[...]
        o = t[:nr // 2] - _lambda_11(lam_ref, lam_init) * t[nr // 2:]
        o_ref[...] = _subln_bf16(o, sg_ref[...], lam_init)


def _attn_sample(page_table, slopes, lam4, qbs, kbs, vbs, sg, ck, cv, *, layer, heads, dec_seq,
                 half, lam_init):
    nb = page_table.shape[0]
    d = qbs.shape[1]
    n_pages = page_table.shape[1]
    pages = PAGES_PER_STEP
    past = n_pages * PAGE_SIZE
    hq = heads * dec_seq
    nr = 2 * hq
    assert nb * (n_pages // pages) >= PAGE_SLOTS - 1 and n_pages % pages == 0
    q4 = qbs.reshape(nb, dec_seq, heads, LANES).transpose(0, 2, 1, 3).reshape(nb, hq, LANES)
    lo = jnp.arange(LANES) < half
    zq = jnp.zeros_like(q4)
    qa = jnp.concatenate([jnp.where(lo, q4, zq), jnp.where(lo, zq, q4)], axis=1)
    za = jnp.zeros_like(qa)
    q2 = jnp.concatenate([jnp.concatenate([qa, za], axis=2),
                          jnp.concatenate([za, qa], axis=2)], axis=1)
    n_new = LANES
    pad = ((0, 0), (0, n_new - hq), (0, 0))
    kn = jnp.pad(kbs.reshape(nb, hq, LANES), pad)
    vn = jnp.pad(vbs.reshape(nb, hq, LANES), pad)
    r = jnp.arange(nr)
    r_head, r_query = (r % hq) // dec_seq, r % dec_seq
    r_slope = slopes[r_head][:, None]

    def table(n_cols, causal):
        c = jnp.arange(n_cols)
        c_head, c_tok = c % heads, c // heads
        ok = c_head[None, :] == r_head[:, None]
        if causal:
            ok = ok & (c_tok[None, :] <= r_query[:, None])
        return jnp.where(ok, r_slope * c_tok[None, :].astype(F32), NEG)

    mb = table(PAGE_SIZE * heads, False)
    mbn = table(n_new, True)

    per_b = lambda b, j, pt: (b, 0, 0)
    const2 = lambda b, j, pt: (0, 0)
    page_rows = PAGE_SIZE * heads
    grid_spec = pltpu.PrefetchScalarGridSpec(
        num_scalar_prefetch=1,
        grid=(nb, n_pages // pages),
        in_specs=[pl.BlockSpec(lam4.shape, const2),
                  pl.BlockSpec((None, 2 * nr, 2 * LANES), per_b),
                  pl.BlockSpec((None, n_new, LANES), per_b),
                  pl.BlockSpec((None, n_new, LANES), per_b),
                  pl.BlockSpec(sg.shape, const2),
                  pl.BlockSpec(mb.shape, const2),
                  pl.BlockSpec(mbn.shape, const2),
                  pl.BlockSpec(r_slope.shape, const2),
                  pl.BlockSpec(memory_space=pl.ANY),
                  pl.BlockSpec(memory_space=pl.ANY)],
        out_specs=pl.BlockSpec((None, hq, LANES), per_b),
        scratch_shapes=[pltpu.VMEM((PAGE_SLOTS, pages, page_rows, LANES), F32),
                        pltpu.VMEM((PAGE_SLOTS, pages, page_rows, LANES), F32),
                        pltpu.SemaphoreType.DMA((PAGE_SLOTS, 2 * pages)),
                        pltpu.VMEM((nr, 1), F32), pltpu.VMEM((nr, 1), F32),
                        pltpu.VMEM((nr, LANES), F32)],
    )
    out = pl.pallas_call(
        functools.partial(_attn_sample_kernel, layer=layer, pages=pages, past=past,
                          lam_init=lam_init),
        grid_spec=grid_spec,
        out_shape=jax.ShapeDtypeStruct((nb, hq, LANES), BF16),
        compiler_params=_params(("arbitrary", "arbitrary")),
        name="attn_sample",
    )(page_table, lam4, q2, kn, vn, sg, mb, mbn, r_slope, ck, cv)
    return out.reshape(nb, heads, dec_seq, LANES).transpose(0, 2, 1, 3).reshape(nb * dec_seq, d)


def _proj_mlp_kernel(x_ref, a_ref, wo_ref, g_ref, wu_ref, wd_ref, o_ref, *, ff_chunk):
    x1 = x_ref[...] + _dot(a_ref[...], wo_ref[...])
    h = _rmsnorm_bf16(x1, g_ref[...])
    acc = x1
    for c in range(wu_ref.shape[1] // ff_chunk):
        sl = slice(c * ff_chunk, (c + 1) * ff_chunk)
        u = jnp.maximum(_dot(h, wu_ref[:, sl]), 0.0)
        acc = acc + _dot((u * u).astype(BF16), wd_ref[sl, :])
    o_ref[...] = acc


def _proj_mlp(x, a, wo, g, wu, wd, *, tm):
    m, d = x.shape
    row = lambda i: (i, 0)
    return pl.pallas_call(
        functools.partial(_proj_mlp_kernel, ff_chunk=min(1024, wu.shape[1])),
        grid=(m // tm,),
        in_specs=[pl.BlockSpec((tm, d), row), pl.BlockSpec((tm, a.shape[1]), row),
                  _resident(wo.shape), _resident((1, d)), _resident(wu.shape), _resident(wd.shape)],
        out_specs=pl.BlockSpec((tm, d), row),
        out_shape=jax.ShapeDtypeStruct((m, d), F32),
        compiler_params=_params(("parallel",)),
        name="proj_mlp",
    )(x, a, wo, g, wu, wd)


def _ret_in_kernel(x_ref, g_ref, w_ref, z_ref, *, n_chunk):
    h = _rmsnorm_bf16(x_ref[...], g_ref[...])
    for c in range(w_ref.shape[1] // n_chunk):
        sl = slice(c * n_chunk, (c + 1) * n_chunk)
        z_ref[:, sl] = _dot(h, w_ref[:, sl]).astype(BF16)


def _ret_in(x, g, w, *, tm):
    m, d = x.shape
    n = w.shape[1]
    row = lambda i: (i, 0)
    return pl.pallas_call(
        functools.partial(_ret_in_kernel, n_chunk=min(1024, n)),
        grid=(m // tm,),
        in_specs=[pl.BlockSpec((tm, d), row), _resident((1, d)), _resident(w.shape)],
        out_specs=pl.BlockSpec((tm, n), row),
        out_shape=jax.ShapeDtypeStruct((m, n), BF16),
        compiler_params=_params(("parallel",)),
        name="ret_in_proj",
    )(x, g, w)


def _ret_tables(lg, c, chunk_len, k_scale):
    idx = lax.broadcasted_iota(jnp.int32, (c, 1), 0).astype(F32)
    diff = (lax.broadcasted_iota(jnp.int32, (c, c), 0)
            - lax.broadcasted_iota(jnp.int32, (c, c), 1)).astype(F32)
    dmat = jnp.where(diff >= 0, jnp.exp(lg * jnp.maximum(diff, 0.0)), 0.0) * k_scale
    q_dec = jnp.exp(lg * (idx + 1.0))
    k_dec = jnp.exp(lg * (chunk_len - 1.0 - idx)) * k_scale
    s_dec = jnp.exp(jnp.full((1, 1), lg * chunk_len, F32))
    return dmat, q_dec, k_dec, s_dec


def _ret_chunk(q, k, v, state, tables):
    dmat, q_dec, k_dec, s_dec = tables
    qk = _dot_nt(q, k) * dmat
    o = _dot(qk.astype(BF16), v) + q_dec * _dot(q, state.astype(BF16))
    kd_t = (k.astype(F32) * k_dec).T.astype(BF16)
    return o, s_dec * state + _dot(kd_t, v)


def _ret_gate(o, g):
    y = o * lax.rsqrt(jnp.mean(o * o, axis=-1, keepdims=True) + EPS)
    g = g.astype(F32)
    return (g * jax.nn.sigmoid(g) * y).astype(BF16)


def _retention_prompt_kernel(lg_ref, q_ref, k_ref, v_ref, g_ref, y_ref, s_ref, *, c, n_chunks,
                             k_scale):
    tables = _ret_tables(lg_ref[pl.program_id(1)], c, c, k_scale)
    s_ref[...] = jnp.zeros(s_ref.shape, F32)

    def chunk(i, _):
        rows = pl.ds(pl.multiple_of(i * c, c), c)
        o, s_ref[...] = _ret_chunk(q_ref[rows, :], k_ref[rows, :], v_ref[rows, :], s_ref[...],
                                   tables)
        y_ref[rows, :] = _ret_gate(o, g_ref[rows, :])
        return 0

    lax.fori_loop(0, n_chunks, chunk, 0, unroll=min(n_chunks, RET_UNROLL))


def _retention_prompt(lg, z, *, n_seq, seq, heads, dk, dv):
    v_off = 2 * heads * dk // dv
    in_specs = [pl.BlockSpec(memory_space=pltpu.SMEM),
                pl.BlockSpec((seq, dk), lambda b, h: (b, h)),
                pl.BlockSpec((seq, dk), lambda b, h: (b, heads + h)),
                pl.BlockSpec((seq, dv), lambda b, h: (b, v_off + h)),
                pl.BlockSpec((seq, dv), lambda b, h: (b, v_off + heads + h))]
    return pl.pallas_call(
        functools.partial(_retention_prompt_kernel, c=min(RET_CHUNK_PROMPT, seq),
                          n_chunks=seq // min(RET_CHUNK_PROMPT, seq), k_scale=dk ** -0.5),
        grid=(n_seq, heads),
        in_specs=in_specs,
        out_specs=[pl.BlockSpec((seq, dv), lambda b, h: (b, h)),
                   pl.BlockSpec((None, None, dk, dv), lambda b, h: (b, h, 0, 0))],
        out_shape=[jax.ShapeDtypeStruct((z.shape[0], heads * dv), BF16),
                   jax.ShapeDtypeStruct((n_seq, heads, dk, dv), F32)],
        compiler_params=_params(("parallel", "parallel")),
        name="retention_prompt",
    )(lg, z, z, z, z)


def _retention_sample_kernel(lg_ref, z_ref, s0_ref, y_ref, s_ref, *, heads, dk, dv, chunk_len,
                             k_scale):
    n_here, rows = z_ref.shape[:2]

    def padded(i, lo, width):
        a = z_ref[i, :, lo:lo + width]
        return jnp.concatenate([a, jnp.zeros((RET_CHUNK - rows, width), a.dtype)], axis=0)

    for h in range(heads):
        tables = _ret_tables(lg_ref[h], RET_CHUNK, chunk_len, k_scale)
        v_lo = 2 * heads * dk + h * dv
        for i in range(n_here):
            o, s_ref[i, h] = _ret_chunk(padded(i, h * dk, dk), padded(i, (heads + h) * dk, dk),
                                        padded(i, v_lo, dv), s0_ref[i, h], tables)
            y_ref[i, :, h * dv:(h + 1) * dv] = _ret_gate(
                o[:rows], z_ref[i, :, v_lo + heads * dv:v_lo + (heads + 1) * dv])


def _retention_sample(lg, z, state, *, chunk_len):
    n_seq, rows, n = z.shape
    _, heads, dk, dv = state.shape
    per = SAMPLE_SEQS_PER_STEP if n_seq % SAMPLE_SEQS_PER_STEP == 0 else 1
    per_b3 = lambda b: (b, 0, 0)
    per_b4 = lambda b: (b, 0, 0, 0)
    return pl.pallas_call(
        functools.partial(_retention_sample_kernel, heads=heads, dk=dk, dv=dv, chunk_len=chunk_len,
                          k_scale=dk ** -0.5),
        grid=(n_seq // per,),
        in_specs=[pl.BlockSpec(memory_space=pltpu.SMEM),
                  pl.BlockSpec((per, rows, n), per_b3),
                  pl.BlockSpec((per, heads, dk, dv), per_b4)],
        out_specs=[pl.BlockSpec((per, rows, heads * dv), per_b3),
                   pl.BlockSpec((per, heads, dk, dv), per_b4)],
        out_shape=[jax.ShapeDtypeStruct((n_seq, rows, heads * dv), BF16),
                   jax.ShapeDtypeStruct(state.shape, F32)],
        compiler_params=_params(("parallel",)),
        name="retention_sample",
    )(lg, z, state)


def _row_tile(m):
    return min(m, 512)


def kernel(x_prompt, x_sample, cache_k, cache_v, state_ret, page_table, norm_mix, norm_ffn,
           w_attn_in, q_norm_g, k_norm_g, lambda_q1, lambda_k1, lambda_q2, lambda_k2, subln_g,
           w_attn_out, w_ret_in, w_ret_out, w_up, w_down):
    batch, seq, d = x_prompt.shape
    dec_batch, dec_seq, _ = x_sample.shape
    depth = norm_mix.shape[0]
    half = q_norm_g.shape[1]
    heads_a = d // (2 * half)
    heads_r, dk, dv = state_ret.shape[2:]
    n_pool = cache_k.shape[1]
    assert 2 * half == LANES and cache_k.shape[2] == PAGE_SIZE and dec_seq & (dec_seq - 1) == 0

    xp = x_prompt.reshape(batch * seq, d)
    xs = x_sample.reshape(dec_batch * dec_seq, d)
    tp, ts = _row_tile(xp.shape[0]), _row_tile(xs.shape[0])
    ck = cache_k.reshape(cache_k.shape[0], n_pool, PAGE_SIZE * heads_a, LANES)
    cv = cache_v.reshape(cache_v.shape[0], n_pool, PAGE_SIZE * heads_a, LANES)
    slopes = 2.0 ** (-8.0 * jnp.arange(1, heads_a + 1, dtype=F32) / heads_a) * LOG2E
    log_gamma = jnp.log1p(-(2.0 ** (-5.0 - jnp.arange(heads_r, dtype=F32))))

    kp_l, vp_l, ks_l, vs_l, sp_l, ss_l = [], [], [], [], [], []
    for i in range(depth):
        g_mix = norm_mix[i].reshape(1, d)
        g_ffn = norm_ffn[i].reshape(1, d)
        wu, wd = w_up[i].astype(BF16), w_down[i].astype(BF16)
        if i % 2 == 0:
            a = i // 2
            lam_init = 0.8 - 0.6 * math.exp(-0.3 * i)
            w_in, wo = w_attn_in[a].astype(BF16), w_attn_out[a].astype(BF16)
            qg = jnp.tile(q_norm_g[a], d // half).reshape(1, d)
            kg = jnp.tile(k_norm_g[a], d // half).reshape(1, d)
            sg = subln_g[a].reshape(1, LANES)
            lam4 = jnp.stack([lambda_q1[a], lambda_k1[a], lambda_q2[a], lambda_k2[a]])
            qb, kf, kb, vf, vb = _qkv(xp, g_mix, w_in, qg, kg, tm=tp, half=half)
            qbs, kfs, kbs, vfs, vbs = _qkv(xs, g_mix, w_in, qg, kg, tm=ts, half=half)
            op = _attn_prompt(slopes, lam4, qb, kb, vb, sg, batch=batch, seq=seq, heads=heads_a,
                              half=half, lam_init=lam_init, tile=min(2048, seq))
            os_ = _attn_sample(page_table, slopes, lam4, qbs, kbs, vbs, sg, ck, cv, layer=a,
                               heads=heads_a, dec_seq=dec_seq, half=half, lam_init=lam_init)
            kp_l.append(kf.reshape(batch, seq, heads_a, LANES))
            vp_l.append(vf.reshape(batch, seq, heads_a, LANES))
            ks_l.append(kfs.reshape(dec_batch, dec_seq, heads_a, LANES))
            vs_l.append(vfs.reshape(dec_batch, dec_seq, heads_a, LANES))
        else:
            r = i // 2
            w_in, wo = w_ret_in[r].astype(BF16), w_ret_out[r].astype(BF16)
            zp = _ret_in(xp, g_mix, w_in, tm=tp)
            zs = _ret_in(xs, g_mix, w_in, tm=ts)
            op, sp = _retention_prompt(log_gamma, zp, n_seq=batch, seq=seq, heads=heads_r,
                                       dk=dk, dv=dv)
            zs = jnp.pad(zs.reshape(dec_batch, dec_seq, -1),
                         ((0, 0), (0, SAMPLE_ROWS - dec_seq), (0, 0)))
            os_, ss = _retention_sample(log_gamma, zs, state_ret[r], chunk_len=dec_seq)
            os_ = os_[:, :dec_seq].reshape(dec_batch * dec_seq, -1)
            sp_l.append(sp)
            ss_l.append(ss)
        xp = _proj_mlp(xp, op, wo, g_ffn, wu, wd, tm=tp)
        xs = _proj_mlp(xs, os_, wo, g_ffn, wu, wd, tm=ts)

    return (xp.reshape(batch, seq, d), xs.reshape(dec_batch, dec_seq, d),
            jnp.stack(kp_l), jnp.stack(vp_l), jnp.stack(ks_l), jnp.stack(vs_l),
            jnp.stack(sp_l), jnp.stack(ss_l))
```

```python
import functools
import math

import jax
import jax.numpy as jnp
from jax import lax
from jax.experimental import pallas as pl
from jax.experimental.pallas import tpu as pltpu

F32 = jnp.float32
BF16 = jnp.bfloat16
EPS = 1e-6
NEG = -1e30
LANES = 128
VMEM_LIMIT = 56 * 1024 * 1024
PAGES_PER_STEP = 8
PAGE_SLOTS = 4
RET_CHUNK = 128
RET_CHUNK_PROMPT = 256
RET_UNROLL = 8
PAGE_SIZE = 128
ATTN_KEY_STEP = 256
SAMPLE_SEQS_PER_STEP = 4
SAMPLE_ROWS = 16
LOG2E = math.log2(math.e)


def _params(sem):
    return pltpu.CompilerParams(dimension_semantics=sem, vmem_limit_bytes=VMEM_LIMIT)


def _resident(shape):
    nd = len(shape)
    return pl.BlockSpec(shape, lambda *_: (0,) * nd, pipeline_mode=pl.Buffered(1))


def _rmsnorm_bf16(x, g):
    return (x * lax.rsqrt(jnp.mean(x * x, axis=-1, keepdims=True) + EPS) * g).astype(BF16)


def _dot(a, b):
    return jnp.dot(a, b, preferred_element_type=F32)


def _dot_nt(a, b):
    return lax.dot_general(a, b, (((1,), (1,)), ((), ())), preferred_element_type=F32)


def _qkv_kernel(x_ref, g_ref, w_ref, qg_ref, kg_ref, qb_ref, kf_ref, kb_ref, vf_ref, vb_ref,
                *, d, half, q_scale):
    h = _rmsnorm_bf16(x_ref[...], g_ref[...])
    tm = h.shape[0]
    n_heads = d // LANES
    lo = lax.broadcasted_iota(jnp.int32, (tm, LANES), 1) < half
    inv = 1.0 / half

    def half_norm(a, gain):
        sq = a * a
        s_lo = jnp.sum(jnp.where(lo, sq, 0.0), axis=-1, keepdims=True)
        s_hi = jnp.sum(jnp.where(lo, 0.0, sq), axis=-1, keepdims=True)
        r = jnp.where(lo, lax.rsqrt(s_lo * inv + EPS), lax.rsqrt(s_hi * inv + EPS))
        return a * r * gain

    q = _dot(h, w_ref[:, 0:d])
    for c in range(d // LANES):
        sl = slice(c * LANES, (c + 1) * LANES)
        qb_ref[:, sl] = (half_norm(q[:, sl], qg_ref[:, sl]) * q_scale).astype(BF16)
    k = _dot(h, w_ref[:, d:2 * d])
    for c in range(d // LANES):
        sl = slice(c * LANES, (c + 1) * LANES)
        kn = half_norm(k[:, sl], kg_ref[:, sl])
        kf_ref[pl.ds(c, tm, stride=n_heads), :] = kn
        kb_ref[:, sl] = kn.astype(BF16)
    v = _dot(h, w_ref[:, 2 * d:3 * d])
    for c in range(n_heads):
        vf_ref[pl.ds(c, tm, stride=n_heads), :] = v[:, c * LANES:(c + 1) * LANES]
    vb_ref[...] = v.astype(BF16)


def _qkv(x, g, w, qg, kg, *, tm, half):
    m, d = x.shape
    row = lambda i: (i, 0)
    n_heads = d // LANES
    bf = jax.ShapeDtypeStruct((m, d), BF16)
    ff = jax.ShapeDtypeStruct((m * n_heads, LANES), F32)
    bf_spec = pl.BlockSpec((tm, d), row)
    ff_spec = pl.BlockSpec((tm * n_heads, LANES), row)
    outs = [bf, ff, bf, ff, bf]
    return pl.pallas_call(
        functools.partial(_qkv_kernel, d=d, half=half, q_scale=half ** -0.5 * LOG2E),
        grid=(m // tm,),
        in_specs=[pl.BlockSpec((tm, d), row), _resident((1, d)), _resident(w.shape),
                  _resident((1, d)), _resident((1, d))],
        out_specs=[bf_spec, ff_spec, bf_spec, ff_spec, bf_spec],
        out_shape=outs,
        compiler_params=_params(("parallel",)),
        name="qkv_proj",
    )(x, g, w, qg, kg)


def _lambda_11(l_ref, lam_init):
    a = jnp.sum(l_ref[0:1, :] * l_ref[1:2, :], axis=-1, keepdims=True)
    b = jnp.sum(l_ref[2:3, :] * l_ref[3:4, :], axis=-1, keepdims=True)
    return jnp.exp(a) - jnp.exp(b) + lam_init


def _subln_bf16(o, sg, lam_init):
    r = lax.rsqrt(jnp.mean(o * o, axis=-1, keepdims=True) + EPS)
    return (o * r * sg * (1.0 - lam_init)).astype(BF16)


def _attn_prompt_kernel(slope_ref, lam_ref, q_ref, k_ref, v_ref, sg_ref, mask_ref, o_ref,
                        qt_scr, vt_scr, *, seq, tile, kstep, half, lam_init):
    slope = slope_ref[pl.program_id(1)]
    lam = _lambda_11(lam_ref, lam_init)
    extra = vt_scr.shape[0] - LANES
    qt_scr[...] = q_ref[...].astype(F32).T.astype(BF16)
    vt_scr[0:LANES, :] = v_ref[...].astype(F32).T.astype(BF16)
    vt_scr[LANES:, :] = jnp.ones((extra, seq), BF16)
    lo = lax.broadcasted_iota(jnp.int32, (LANES, tile), 0) < half
    zero = jnp.zeros((), BF16)
    lane = lax.broadcasted_iota(jnp.int32, (kstep, LANES), 1)
    b0 = slope * lax.broadcasted_iota(jnp.int32, (kstep, LANES), 0).astype(F32)
    b_hi = b0.astype(BF16).astype(F32)
    b_mid = (b0 - b_hi).astype(BF16).astype(F32)
    b_lo = b0 - b_hi - b_mid
    k_extra = jnp.where(lane == 0, b_hi, jnp.where(lane == 1, b_mid,
                                                   jnp.where(lane == 2, b_lo, 0.0))).astype(BF16)
    q_extra = jnp.where(lax.broadcasted_iota(jnp.int32, (LANES, 2 * tile), 0) < 3,
                        1.0, 0.0).astype(BF16)

    for qi in range(seq // tile):
        q0 = qi * tile
        qt = qt_scr[:, q0:q0 + tile]
        c1, c2 = jnp.where(lo, qt, zero), jnp.where(lo, zero, qt)
        q2t = jnp.concatenate(
            [c[:, j:j + kstep] for j in range(0, tile, kstep) for c in (c1, c2)], axis=1)
        q2t = jnp.concatenate([q2t, q_extra], axis=0)
        m = jnp.full((1, 2 * tile), NEG, F32)
        acc = jnp.zeros((LANES + extra, 2 * tile), F32)
        for k0 in range(0, q0 + tile, kstep):
            first = 2 * max(k0 - q0, 0)
            s = _dot(jnp.concatenate([k_ref[k0:k0 + kstep, :], k_extra], axis=1), q2t[:, first:])
            if k0 >= q0:
                blk = s[:, :2 * kstep] + mask_ref[...]
                s = blk if first + 2 * kstep == 2 * tile else jnp.concatenate(
                    [blk, s[:, 2 * kstep:]], axis=1)
            c = slope * float(k0 - q0)
            m_old = m[:, first:]
            m_new = jnp.maximum(m_old, jnp.max(s, axis=0, keepdims=True) + c)
            alpha = jnp.exp2(m_old - m_new)
            p = jnp.exp2(s - (m_new - c))
            acc_new = alpha * acc[:, first:] + _dot(vt_scr[:, k0:k0 + kstep], p.astype(BF16))
            if first:
                m_new = jnp.concatenate([m[:, :first], m_new], axis=1)
                acc_new = jnp.concatenate([acc[:, :first], acc_new], axis=1)
            m, acc = m_new, acc_new
        t = acc[:LANES] / acc[LANES:LANES + 1]
        o = [t[:, j:j + kstep] - lam * t[:, j + kstep:j + 2 * kstep]
             for j in range(0, 2 * tile, 2 * kstep)]
        o = o[0] if len(o) == 1 else jnp.concatenate(o, axis=1)
        r = lax.rsqrt(jnp.mean(o * o, axis=0, keepdims=True) + EPS)
        o_ref[q0:q0 + tile, :] = ((o * r).T * sg_ref[...] * (1.0 - lam_init)).astype(BF16)


def _attn_prompt(slopes, lam4, qb, kb, vb, sg, *, batch, seq, heads, half, lam_init, tile):
    m, d = qb.shape
    blk = pl.BlockSpec((seq, LANES), lambda b, h: (b, h))
    kstep = min(ATTN_KEY_STEP, tile)
    key = jnp.arange(kstep)[:, None]
    qry = jnp.arange(2 * kstep)[None, :] % kstep
    mask = jnp.where(key <= qry, 0.0, NEG).astype(F32)
    return pl.pallas_call(
        functools.partial(_attn_prompt_kernel, seq=seq, tile=tile, kstep=kstep,
                          half=half, lam_init=lam_init),
        grid=(batch, heads),
        in_specs=[pl.BlockSpec(memory_space=pltpu.SMEM), _resident(lam4.shape),
                  blk, blk, blk, _resident(sg.shape), _resident(mask.shape)],
        out_specs=blk,
        out_shape=jax.ShapeDtypeStruct((m, d), BF16),
        scratch_shapes=[pltpu.VMEM((LANES, seq), BF16), pltpu.VMEM((LANES + 16, seq), BF16)],
        compiler_params=_params(("parallel", "parallel")),
        name="attn_prompt",
    )(slopes, lam4, qb, kb, vb, sg, mask)


def _page_part(s):
    m = jnp.max(s, axis=-1, keepdims=True)
    p = jnp.exp2(s - m)
    return m, jnp.sum(p, axis=-1, keepdims=True), p


def _attn_sample_kernel(pt_ref, lam_ref, q2_ref, kn_ref, vn_ref, sg_ref, mb_ref, mbn_ref, sc_ref,
                        ck_ref, cv_ref, o_ref, k_buf, v_buf, sem, m_scr, l_scr, acc_scr,
                        *, layer, pages, past, lam_init):
    j = pl.program_id(1)
    n_j = pl.num_programs(1)
    step = pl.program_id(0) * n_j + j
    n_steps = pl.num_programs(0) * n_j
    nr = mb_ref.shape[0]

    def page_copies(s):
        slot = s % PAGE_SLOTS
        sb, sj = s // n_j, s % n_j
        out = []
        for i in range(pages):
            page = pt_ref[sb, sj * pages + i]
            out.append(pltpu.make_async_copy(ck_ref.at[layer, page], k_buf.at[slot, i],
                                             sem.at[slot, i]))
            out.append(pltpu.make_async_copy(cv_ref.at[layer, page], v_buf.at[slot, i],
                                             sem.at[slot, pages + i]))
        return out

    @pl.when(step == 0)
    def _():
        for s in range(PAGE_SLOTS - 1):
            for c in page_copies(s):
                c.start()

    @pl.when(step + PAGE_SLOTS - 1 < n_steps)
    def _():
        for c in page_copies(step + PAGE_SLOTS - 1):
            c.start()

    @pl.when(j == 0)
    def _():
        m_scr[...] = jnp.full(m_scr.shape, NEG, F32)
        l_scr[...] = jnp.zeros(l_scr.shape, F32)
        acc_scr[...] = jnp.zeros(acc_scr.shape, F32)

    def merge(parts):
        m_old = m_scr[...]
        m_new = m_old
        for m, _, _ in parts:
            m_new = jnp.maximum(m_new, m)
        a = jnp.exp2(m_old - m_new)
        l = a * l_scr[...]
        acc = a * acc_scr[...]
        for m, lp, op in parts:
            w = jnp.exp2(m - m_new)
            l = l + w * lp
            acc = acc + w * op
        m_scr[...] = m_new
        l_scr[...] = l
        acc_scr[...] = acc

    for c in page_copies(step):
        c.wait()
    slot = step % PAGE_SLOTS
    q2 = q2_ref[...]
    mb = mb_ref[...]
    sc = sc_ref[...]
    parts = []
    for pr in range(pages // 2):
        ia, ib = 2 * pr, 2 * pr + 1
        kcat = jnp.concatenate([k_buf[slot, ia].astype(BF16), k_buf[slot, ib].astype(BF16)], axis=1)
        s2 = _dot_nt(q2, kcat)
        ma, la, pa = _page_part(s2[:nr] + mb)
        mb_, lb, pb = _page_part(s2[nr:] + mb)
        vcat = jnp.concatenate([v_buf[slot, ia].astype(BF16), v_buf[slot, ib].astype(BF16)], axis=1)
        o2 = _dot(jnp.concatenate([pa, pb], axis=0).astype(BF16), vcat)
        off = ((j * pages + ia) * PAGE_SIZE - past).astype(F32)
        parts.append((ma + sc * off, la, o2[:nr, :LANES]))
        parts.append((mb_ + sc * (off + PAGE_SIZE), lb, o2[nr:, LANES:]))
    merge(parts)

    @pl.when(j == n_j - 1)
    def _():
        mn, ln, pn = _page_part(_dot_nt(q2[:nr, :LANES], kn_ref[...]) + mbn_ref[...])
        merge([(mn, ln, _dot(pn.astype(BF16), vn_ref[...]))])
        t = acc_scr[...] / l_scr[...]
        o = t[:nr // 2] - _lambda_11(lam_ref, lam_init) * t[nr // 2:]
        o_ref[...] = _subln_bf16(o, sg_ref[...], lam_init)


def _attn_sample(page_table, slopes, lam4, qbs, kbs, vbs, sg, ck, cv, *, layer, heads, dec_seq,
                 half, lam_init):
    nb = page_table.shape[0]
    d = qbs.shape[1]
    n_pages = page_table.shape[1]
    pages = PAGES_PER_STEP
    past = n_pages * PAGE_SIZE
    hq = heads * dec_seq
    nr = 2 * hq
    assert nb * (n_pages // pages) >= PAGE_SLOTS - 1 and n_pages % pages == 0
    q4 = qbs.reshape(nb, dec_seq, heads, LANES).transpose(0, 2, 1, 3).reshape(nb, hq, LANES)
    lo = jnp.arange(LANES) < half
    zq = jnp.zeros_like(q4)
    qa = jnp.concatenate([jnp.where(lo, q4, zq), jnp.where(lo, zq, q4)], axis=1)
    za = jnp.zeros_like(qa)
    q2 = jnp.concatenate([jnp.concatenate([qa, za], axis=2),
                          jnp.concatenate([za, qa], axis=2)], axis=1)
    n_new = LANES
    pad = ((0, 0), (0, n_new - hq), (0, 0))
    kn = jnp.pad(kbs.reshape(nb, hq, LANES), pad)
    vn = jnp.pad(vbs.reshape(nb, hq, LANES), pad)
    r = jnp.arange(nr)
    r_head, r_query = (r % hq) // dec_seq, r % dec_seq
    r_slope = slopes[r_head][:, None]

    def table(n_cols, causal):
        c = jnp.arange(n_cols)
        c_head, c_tok = c % heads, c // heads
        ok = c_head[None, :] == r_head[:, None]
        if causal:
            ok = ok & (c_tok[None, :] <= r_query[:, None])
        return jnp.where(ok, r_slope * c_tok[None, :].astype(F32), NEG)

    mb = table(PAGE_SIZE * heads, False)
    mbn = table(n_new, True)

    per_b = lambda b, j, pt: (b, 0, 0)
    const2 = lambda b, j, pt: (0, 0)
    page_rows = PAGE_SIZE * heads
    grid_spec = pltpu.PrefetchScalarGridSpec(
        num_scalar_prefetch=1,
        grid=(nb, n_pages // pages),
        in_specs=[pl.BlockSpec(lam4.shape, const2),
                  pl.BlockSpec((None, 2 * nr, 2 * LANES), per_b),
                  pl.BlockSpec((None, n_new, LANES), per_b),
                  pl.BlockSpec((None, n_new, LANES), per_b),
                  pl.BlockSpec(sg.shape, const2),
                  pl.BlockSpec(mb.shape, const2),
                  pl.BlockSpec(mbn.shape, const2),
                  pl.BlockSpec(r_slope.shape, const2),
                  pl.BlockSpec(memory_space=pl.ANY),
                  pl.BlockSpec(memory_space=pl.ANY)],
        out_specs=pl.BlockSpec((None, hq, LANES), per_b),
        scratch_shapes=[pltpu.VMEM((PAGE_SLOTS, pages, page_rows, LANES), F32),
                        pltpu.VMEM((PAGE_SLOTS, pages, page_rows, LANES), F32),
                        pltpu.SemaphoreType.DMA((PAGE_SLOTS, 2 * pages)),
                        pltpu.VMEM((nr, 1), F32), pltpu.VMEM((nr, 1), F32),
                        pltpu.VMEM((nr, LANES), F32)],
    )
    out = pl.pallas_call(
        functools.partial(_attn_sample_kernel, layer=layer, pages=pages, past=past,
                          lam_init=lam_init),
        grid_spec=grid_spec,
        out_shape=jax.ShapeDtypeStruct((nb, hq, LANES), BF16),
        compiler_params=_params(("arbitrary", "arbitrary")),
        name="attn_sample",
    )(page_table, lam4, q2, kn, vn, sg, mb, mbn, r_slope, ck, cv)
    return out.reshape(nb, heads, dec_seq, LANES).transpose(0, 2, 1, 3).reshape(nb * dec_seq, d)


def _proj_mlp_kernel(x_ref, a_ref, wo_ref, g_ref, wu_ref, wd_ref, o_ref, *, ff_chunk):
    x1 = x_ref[...] + _dot(a_ref[...], wo_ref[...])
    h = _rmsnorm_bf16(x1, g_ref[...])
    acc = x1
    for c in range(wu_ref.shape[1] // ff_chunk):
        sl = slice(c * ff_chunk, (c + 1) * ff_chunk)
        u = jnp.maximum(_dot(h, wu_ref[:, sl]), 0.0)
        acc = acc + _dot((u * u).astype(BF16), wd_ref[sl, :])
    o_ref[...] = acc


def _proj_mlp(x, a, wo, g, wu, wd, *, layer, tm):
    m, d = x.shape
    row = lambda i: (i, 0)
    slab = lambda w: pl.BlockSpec((None,) + w.shape[1:], lambda *_: (layer, 0, 0),
                                  pipeline_mode=pl.Buffered(1))
    return pl.pallas_call(
        functools.partial(_proj_mlp_kernel, ff_chunk=min(1024, wu.shape[2])),
        grid=(m // tm,),
        in_specs=[pl.BlockSpec((tm, d), row), pl.BlockSpec((tm, a.shape[1]), row),
                  _resident(wo.shape), _resident((1, d)), slab(wu), slab(wd)],
        out_specs=pl.BlockSpec((tm, d), row),
        out_shape=jax.ShapeDtypeStruct((m, d), F32),
        compiler_params=_params(("parallel",)),
        name="proj_mlp",
    )(x, a, wo, g, wu, wd)


def _ret_in_kernel(x_ref, g_ref, w_ref, z_ref, *, n_chunk):
    h = _rmsnorm_bf16(x_ref[...], g_ref[...])
    for c in range(w_ref.shape[1] // n_chunk):
        sl = slice(c * n_chunk, (c + 1) * n_chunk)
        z_ref[:, sl] = _dot(h, w_ref[:, sl]).astype(BF16)


def _ret_in(x, g, w, *, tm):
    m, d = x.shape
    n = w.shape[1]
    row = lambda i: (i, 0)
    return pl.pallas_call(
        functools.partial(_ret_in_kernel, n_chunk=min(1024, n)),
        grid=(m // tm,),
        in_specs=[pl.BlockSpec((tm, d), row), _resident((1, d)), _resident(w.shape)],
        out_specs=pl.BlockSpec((tm, n), row),
        out_shape=jax.ShapeDtypeStruct((m, n), BF16),
        compiler_params=_params(("parallel",)),
        name="ret_in_proj",
    )(x, g, w)


def _ret_tables(lg, c, chunk_len, k_scale):
    idx = lax.broadcasted_iota(jnp.int32, (c, 1), 0).astype(F32)
    diff = (lax.broadcasted_iota(jnp.int32, (c, c), 0)
            - lax.broadcasted_iota(jnp.int32, (c, c), 1)).astype(F32)
    dmat = jnp.where(diff >= 0, jnp.exp(lg * jnp.maximum(diff, 0.0)), 0.0) * k_scale
    q_dec = jnp.exp(lg * (idx + 1.0))
    k_dec = jnp.exp(lg * (chunk_len - 1.0 - idx)) * k_scale
    s_dec = jnp.exp(jnp.full((1, 1), lg * chunk_len, F32))
    return dmat, q_dec, k_dec, s_dec


def _ret_chunk(q, k, v, state, tables):
    dmat, q_dec, k_dec, s_dec = tables
    qk = _dot_nt(q, k) * dmat
    o = _dot(qk.astype(BF16), v) + q_dec * _dot(q, state.astype(BF16))
    kd_t = (k.astype(F32) * k_dec).T.astype(BF16)
    return o, s_dec * state + _dot(kd_t, v)


def _ret_gate(o, g):
    y = o * lax.rsqrt(jnp.mean(o * o, axis=-1, keepdims=True) + EPS)
    g = g.astype(F32)
    return (g * jax.nn.sigmoid(g) * y).astype(BF16)


def _retention_prompt_kernel(lg_ref, q_ref, k_ref, v_ref, g_ref, y_ref, s_ref, *, c, n_chunks,
                             k_scale):
    tables = _ret_tables(lg_ref[pl.program_id(1)], c, c, k_scale)
    s_ref[...] = jnp.zeros(s_ref.shape, F32)

    def chunk(i, _):
        rows = pl.ds(pl.multiple_of(i * c, c), c)
        o, s_ref[...] = _ret_chunk(q_ref[rows, :], k_ref[rows, :], v_ref[rows, :], s_ref[...],
                                   tables)
        y_ref[rows, :] = _ret_gate(o, g_ref[rows, :])
        return 0

    lax.fori_loop(0, n_chunks, chunk, 0, unroll=min(n_chunks, RET_UNROLL))


def _retention_prompt(lg, z, *, n_seq, seq, heads, dk, dv):
    v_off = 2 * heads * dk // dv
    in_specs = [pl.BlockSpec(memory_space=pltpu.SMEM),
                pl.BlockSpec((seq, dk), lambda b, h: (b, h)),
                pl.BlockSpec((seq, dk), lambda b, h: (b, heads + h)),
                pl.BlockSpec((seq, dv), lambda b, h: (b, v_off + h)),
                pl.BlockSpec((seq, dv), lambda b, h: (b, v_off + heads + h))]
    return pl.pallas_call(
        functools.partial(_retention_prompt_kernel, c=min(RET_CHUNK_PROMPT, seq),
                          n_chunks=seq // min(RET_CHUNK_PROMPT, seq), k_scale=dk ** -0.5),
        grid=(n_seq, heads),
        in_specs=in_specs,
        out_specs=[pl.BlockSpec((seq, dv), lambda b, h: (b, h)),
                   pl.BlockSpec((None, None, dk, dv), lambda b, h: (b, h, 0, 0))],
        out_shape=[jax.ShapeDtypeStruct((z.shape[0], heads * dv), BF16),
                   jax.ShapeDtypeStruct((n_seq, heads, dk, dv), F32)],
        compiler_params=_params(("parallel", "parallel")),
        name="retention_prompt",
    )(lg, z, z, z, z)


def _retention_sample_kernel(lg_ref, z_ref, s0_ref, y_ref, s_ref, *, heads, dk, dv, chunk_len,
                             k_scale):
    n_here, rows = z_ref.shape[:2]

    def padded(i, lo, width):
        a = z_ref[i, :, lo:lo + width]
        return jnp.concatenate([a, jnp.zeros((RET_CHUNK - rows, width), a.dtype)], axis=0)

    for h in range(heads):
        tables = _ret_tables(lg_ref[h], RET_CHUNK, chunk_len, k_scale)
        v_lo = 2 * heads * dk + h * dv
        for i in range(n_here):
            o, s_ref[i, h] = _ret_chunk(padded(i, h * dk, dk), padded(i, (heads + h) * dk, dk),
                                        padded(i, v_lo, dv), s0_ref[i, h], tables)
            y_ref[i, :, h * dv:(h + 1) * dv] = _ret_gate(
                o[:rows], z_ref[i, :, v_lo + heads * dv:v_lo + (heads + 1) * dv])


def _retention_sample(lg, z, state, *, chunk_len):
    n_seq, rows, n = z.shape
    _, heads, dk, dv = state.shape
    per = SAMPLE_SEQS_PER_STEP if n_seq % SAMPLE_SEQS_PER_STEP == 0 else 1
    per_b3 = lambda b: (b, 0, 0)
    per_b4 = lambda b: (b, 0, 0, 0)
    return pl.pallas_call(
        functools.partial(_retention_sample_kernel, heads=heads, dk=dk, dv=dv, chunk_len=chunk_len,
                          k_scale=dk ** -0.5),
        grid=(n_seq // per,),
        in_specs=[pl.BlockSpec(memory_space=pltpu.SMEM),
                  pl.BlockSpec((per, rows, n), per_b3),
                  pl.BlockSpec((per, heads, dk, dv), per_b4)],
        out_specs=[pl.BlockSpec((per, rows, heads * dv), per_b3),
                   pl.BlockSpec((per, heads, dk, dv), per_b4)],
        out_shape=[jax.ShapeDtypeStruct((n_seq, rows, heads * dv), BF16),
                   jax.ShapeDtypeStruct(state.shape, F32)],
        compiler_params=_params(("parallel",)),
        name="retention_sample",
    )(lg, z, state)


def _row_tile(m):
    return min(m, 512)


def kernel(x_prompt, x_sample, cache_k, cache_v, state_ret, page_table, norm_mix, norm_ffn,
           w_attn_in, q_norm_g, k_norm_g, lambda_q1, lambda_k1, lambda_q2, lambda_k2, subln_g,
           w_attn_out, w_ret_in, w_ret_out, w_up, w_down):
    batch, seq, d = x_prompt.shape
    dec_batch, dec_seq, _ = x_sample.shape
    depth = norm_mix.shape[0]
    half = q_norm_g.shape[1]
    heads_a = d // (2 * half)
    heads_r, dk, dv = state_ret.shape[2:]
    n_pool = cache_k.shape[1]
    assert 2 * half == LANES and cache_k.shape[2] == PAGE_SIZE and dec_seq & (dec_seq - 1) == 0

    xp = x_prompt.reshape(batch * seq, d)
    xs = x_sample.reshape(dec_batch * dec_seq, d)
    tp, ts = _row_tile(xp.shape[0]), _row_tile(xs.shape[0])
    ck = cache_k.reshape(cache_k.shape[0], n_pool, PAGE_SIZE * heads_a, LANES)
    cv = cache_v.reshape(cache_v.shape[0], n_pool, PAGE_SIZE * heads_a, LANES)
    slopes = 2.0 ** (-8.0 * jnp.arange(1, heads_a + 1, dtype=F32) / heads_a) * LOG2E
    log_gamma = jnp.log1p(-(2.0 ** (-5.0 - jnp.arange(heads_r, dtype=F32))))

    wu, wd = w_up.astype(BF16), w_down.astype(BF16)
    kp_l, vp_l, ks_l, vs_l, sp_l, ss_l = [], [], [], [], [], []
    for i in range(depth):
        g_mix = norm_mix[i].reshape(1, d)
        g_ffn = norm_ffn[i].reshape(1, d)
        if i % 2 == 0:
            a = i // 2
            lam_init = 0.8 - 0.6 * math.exp(-0.3 * i)
            w_in, wo = w_attn_in[a].astype(BF16), w_attn_out[a].astype(BF16)
            qg = jnp.tile(q_norm_g[a], d // half).reshape(1, d)
            kg = jnp.tile(k_norm_g[a], d // half).reshape(1, d)
            sg = subln_g[a].reshape(1, LANES)
            lam4 = jnp.stack([lambda_q1[a], lambda_k1[a], lambda_q2[a], lambda_k2[a]])
            qb, kf, kb, vf, vb = _qkv(xp, g_mix, w_in, qg, kg, tm=tp, half=half)
            qbs, kfs, kbs, vfs, vbs = _qkv(xs, g_mix, w_in, qg, kg, tm=ts, half=half)
            op = _attn_prompt(slopes, lam4, qb, kb, vb, sg, batch=batch, seq=seq, heads=heads_a,
                              half=half, lam_init=lam_init, tile=min(2048, seq))
            os_ = _attn_sample(page_table, slopes, lam4, qbs, kbs, vbs, sg, ck, cv, layer=a,
                               heads=heads_a, dec_seq=dec_seq, half=half, lam_init=lam_init)
            kp_l.append(kf.reshape(batch, seq, heads_a, LANES))
            vp_l.append(vf.reshape(batch, seq, heads_a, LANES))
            ks_l.append(kfs.reshape(dec_batch, dec_seq, heads_a, LANES))
            vs_l.append(vfs.reshape(dec_batch, dec_seq, heads_a, LANES))
        else:
            r = i // 2
            w_in, wo = w_ret_in[r].astype(BF16), w_ret_out[r].astype(BF16)
            zp = _ret_in(xp, g_mix, w_in, tm=tp)
            zs = _ret_in(xs, g_mix, w_in, tm=ts)
            op, sp = _retention_prompt(log_gamma, zp, n_seq=batch, seq=seq, heads=heads_r,
                                       dk=dk, dv=dv)
            zs = jnp.pad(zs.reshape(dec_batch, dec_seq, -1),
                         ((0, 0), (0, SAMPLE_ROWS - dec_seq), (0, 0)))
            os_, ss = _retention_sample(log_gamma, zs, state_ret[r], chunk_len=dec_seq)
            os_ = os_[:, :dec_seq].reshape(dec_batch * dec_seq, -1)
            sp_l.append(sp)
            ss_l.append(ss)
        xp = _proj_mlp(xp, op, wo, g_ffn, wu, wd, layer=i, tm=tp)
        xs = _proj_mlp(xs, os_, wo, g_ffn, wu, wd, layer=i, tm=ts)

    return (xp.reshape(batch, seq, d), xs.reshape(dec_batch, dec_seq, d),
            jnp.stack(kp_l), jnp.stack(vp_l), jnp.stack(ks_l), jnp.stack(vs_l),
            jnp.stack(sp_l), jnp.stack(ss_l))
```
